```python
import jax, jax.numpy as jnp
from jax import lax
import numpy as np

D_MODEL = 2048
BATCH = 4
SEQ = 4096
DEPTH = 1
DEC_BATCH = 1
DEC_SEQ = 16384
PAST_LEN = 128

MLA_HEADS = 8
QK_NOPE = 128
QK_ROPE = 64
V_HEAD = 128
Q_LORA = 512
KV_LORA = 512
ROPE_THETA = 10000.0
Q_BLOCK = 128
GLA_HEADS = 4
GLA_DK = 128
GLA_DV = 256
GLA_GATE_RANK = 16
GLA_GATE_NORM = 16.0
GLA_CHUNK = 64
D_FF = 4 * D_MODEL
MIX_WIDTH = MLA_HEADS * V_HEAD + GLA_HEADS * GLA_DV
EPS = 1e-6

IN_SPLITS = (Q_LORA, KV_LORA, QK_ROPE, GLA_HEADS * GLA_DK, GLA_HEADS * GLA_DK, GLA_HEADS * GLA_DV, GLA_GATE_RANK, GLA_GATE_RANK, GLA_HEADS * GLA_DV)
IN_WIDTH = sum(IN_SPLITS)

kernel_name = 'hymba_mla_gla_sandwich_encoder'


def rms_norm(x, w):
    xf = x.astype(jnp.float32)
    y = xf * lax.rsqrt(jnp.mean(xf * xf, axis=-1, keepdims=True) + EPS)
    return (y * w.astype(jnp.float32)).astype(x.dtype)


def rope_tables(L):
    inv = 1.0 / (ROPE_THETA ** (jnp.arange(0, QK_ROPE, 2, dtype=jnp.float32) / QK_ROPE))
    ang = jnp.arange(L, dtype=jnp.float32)[:, None] * inv[None, :]
    return jnp.cos(ang), jnp.sin(ang)


def apply_rope(x, cos, sin):
    xf = x.astype(jnp.float32)
    x1, x2 = xf[..., :QK_ROPE // 2], xf[..., QK_ROPE // 2:]
    return jnp.concatenate([x1 * cos - x2 * sin, x2 * cos + x1 * sin], axis=-1).astype(x.dtype)


def mla_attention(c_q, c_kv, k_rope, q_a_norm, w_q_b, kv_a_norm, w_kv_b):
    B, L, _ = c_q.shape
    cos, sin = rope_tables(L)
    q = (rms_norm(c_q, q_a_norm) @ w_q_b).reshape(B, L, MLA_HEADS, QK_NOPE + QK_ROPE)
    q_nope = q[..., :QK_NOPE]
    q_rope = apply_rope(q[..., QK_NOPE:], cos[:, None, :], sin[:, None, :])
    kv = (rms_norm(c_kv, kv_a_norm) @ w_kv_b).reshape(B, L, MLA_HEADS, QK_NOPE + V_HEAD)
    k_nope, v = kv[..., :QK_NOPE], kv[..., QK_NOPE:]
    k_pe = apply_rope(k_rope, cos, sin)
    scale = (QK_NOPE + QK_ROPE) ** -0.5
    nb = L // Q_BLOCK
    qn_blocks = q_nope.reshape(B, nb, Q_BLOCK, MLA_HEADS, QK_NOPE).transpose(1, 0, 2, 3, 4)
    qr_blocks = q_rope.reshape(B, nb, Q_BLOCK, MLA_HEADS, QK_ROPE).transpose(1, 0, 2, 3, 4)

    def attend(blk):
        qn, qr = blk
        s = jnp.einsum('bqhd,bkhd->bhqk', qn, k_nope) + jnp.einsum('bqhr,bkr->bhqk', qr, k_pe)
        p = jax.nn.softmax(s.astype(jnp.float32) * scale, axis=-1).astype(v.dtype)
        return jnp.einsum('bhqk,bkhd->bqhd', p, v)

    o = lax.map(attend, (qn_blocks, qr_blocks))
    return o.transpose(1, 0, 2, 3, 4).reshape(B, L, MLA_HEADS * V_HEAD)


def gla_scan(q, k, v, log_g):
    B, L, H, DK = q.shape
    DV = v.shape[-1]
    N = L // GLA_CHUNK

    def to_chunks(t):
        return t.reshape(B, N, GLA_CHUNK, H, t.shape[-1]).transpose(1, 0, 3, 2, 4)

    qc, kc, vc, gc = to_chunks(q), to_chunks(k), to_chunks(v), to_chunks(log_g)
    causal_in_chunk = jnp.tril(jnp.ones((GLA_CHUNK, GLA_CHUNK), dtype=bool))

    def step(S, xs):
        qi, ki, vi, gi = xs
        b = jnp.cumsum(gi, axis=2)
        b_last = b[:, :, -1:, :]
        q_dec = qi * jnp.exp(b)
        k_inv = ki * jnp.exp(-b)
        k_tail = ki * jnp.exp(b_last - b)
        a = jnp.where(causal_in_chunk, jnp.einsum('bhid,bhjd->bhij', q_dec, k_inv), 0.0)
        o = jnp.einsum('bhij,bhjv->bhiv', a, vi) + jnp.einsum('bhid,bhdv->bhiv', q_dec, S)
        S = S * jnp.exp(b_last)[:, :, 0, :, None] + jnp.einsum('bhjd,bhjv->bhdv', k_tail, vi)
        return S, o

    S0 = jnp.zeros((B, H, DK, DV), jnp.float32)
    _, o = lax.scan(step, S0, (qc, kc, vc, gc))
    return o.transpose(1, 0, 3, 2, 4).reshape(B, L, H, DV)


def gla_mixer(gq, gk, gv, gf, gb, og, w_gk_f, b_gk_f, w_gk_b, b_gk_b, gla_norm):
    B, L, _ = gq.shape
    f32 = jnp.float32
    qh = gq.astype(f32).reshape(B, L, GLA_HEADS, GLA_DK) * (GLA_DK ** -0.5)
    kh = gk.astype(f32).reshape(B, L, GLA_HEADS, GLA_DK)
    vh = gv.astype(f32).reshape(B, L, GLA_HEADS, GLA_DV)
    lg_f = (jax.nn.log_sigmoid((gf @ w_gk_f + b_gk_f).astype(f32)) / GLA_GATE_NORM).reshape(B, L, GLA_HEADS, GLA_DK)
    lg_b = (jax.nn.log_sigmoid((gb @ w_gk_b + b_gk_b).astype(f32)) / GLA_GATE_NORM).reshape(B, L, GLA_HEADS, GLA_DK)
    flip = lambda t: jnp.flip(t, axis=1)
    o = gla_scan(qh, kh, vh, lg_f) + flip(gla_scan(flip(qh), flip(kh), flip(vh), flip(lg_b)))
    o = rms_norm(o, gla_norm)
    o = o * jax.nn.silu(og.astype(f32)).reshape(B, L, GLA_HEADS, GLA_DV)
    return o.reshape(B, L, GLA_HEADS * GLA_DV).astype(gq.dtype)


def encoder_layer(x, w_in, q_a_norm, w_q_b, kv_a_norm, w_kv_b, w_gk_f, b_gk_f, w_gk_b, b_gk_b, gla_norm, w_out, pre_mix_norm, post_mix_norm, pre_mlp_norm, post_mlp_norm, w_up, w_down):
    split_points = np.cumsum(IN_SPLITS)[:-1].tolist()
    h = rms_norm(x, pre_mix_norm)
    c_q, c_kv, k_rope, gq, gk, gv, gf, gb, og = jnp.split(h @ w_in, split_points, axis=-1)
    mla_out = mla_attention(c_q, c_kv, k_rope, q_a_norm, w_q_b, kv_a_norm, w_kv_b)
    gla_out = gla_mixer(gq, gk, gv, gf, gb, og, w_gk_f, b_gk_f, w_gk_b, b_gk_b, gla_norm)
    mix = jnp.concatenate([mla_out, gla_out], axis=-1) @ w_out
    x = x + rms_norm(mix, post_mix_norm)
    h = rms_norm(x, pre_mlp_norm)
    f = jnp.square(jax.nn.relu(h @ w_up)) @ w_down
    return x + rms_norm(f, post_mlp_norm)


def trunk(x, w_in, q_a_norm, w_q_b, kv_a_norm, w_kv_b, w_gk_f, b_gk_f, w_gk_b, b_gk_b, gla_norm, w_out, pre_mix_norm, post_mix_norm, pre_mlp_norm, post_mlp_norm, w_up, w_down):
    for l in range(DEPTH):
        x = encoder_layer(x, w_in[l], q_a_norm[l], w_q_b[l], kv_a_norm[l], w_kv_b[l], w_gk_f[l], b_gk_f[l], w_gk_b[l], b_gk_b[l], gla_norm[l], w_out[l], pre_mix_norm[l], post_mix_norm[l], pre_mlp_norm[l], post_mlp_norm[l], w_up[l], w_down[l])
    return x


def setup_inputs(seed: int = 0) -> dict:
    key = jax.random.key(seed)
    ks = jax.random.split(key, 19)
    f32 = jnp.float32
    nrm = lambda k, shape, fan_in: jax.random.normal(k, shape, f32) * (fan_in ** -0.5)
    gain = lambda k, n: 1.0 + 0.02 * jax.random.normal(k, (DEPTH, n), f32)
    return {
        'x_prompt': jax.random.normal(ks[0], (BATCH, SEQ, D_MODEL), f32),
        'x_sample': jax.random.normal(ks[1], (DEC_BATCH, DEC_SEQ, D_MODEL), f32),
        'w_in': nrm(ks[2], (DEPTH, D_MODEL, IN_WIDTH), D_MODEL),
        'q_a_norm': gain(ks[3], Q_LORA),
        'w_q_b': nrm(ks[4], (DEPTH, Q_LORA, MLA_HEADS * (QK_NOPE + QK_ROPE)), Q_LORA),
        'kv_a_norm': gain(ks[5], KV_LORA),
        'w_kv_b': nrm(ks[6], (DEPTH, KV_LORA, MLA_HEADS * (QK_NOPE + V_HEAD)), KV_LORA),
        'w_gk_f': nrm(ks[7], (DEPTH, GLA_GATE_RANK, GLA_HEADS * GLA_DK), GLA_GATE_RANK),
        'b_gk_f': 0.1 * jax.random.normal(ks[8], (DEPTH, GLA_HEADS * GLA_DK), f32),
        'w_gk_b': nrm(ks[9], (DEPTH, GLA_GATE_RANK, GLA_HEADS * GLA_DK), GLA_GATE_RANK),
        'b_gk_b': 0.1 * jax.random.normal(ks[10], (DEPTH, GLA_HEADS * GLA_DK), f32),
        'gla_norm': gain(ks[11], GLA_DV),
        'w_out': nrm(ks[12], (DEPTH, MIX_WIDTH, D_MODEL), MIX_WIDTH),
        'pre_mix_norm': gain(ks[13], D_MODEL),
        'post_mix_norm': gain(ks[14], D_MODEL),
        'pre_mlp_norm': gain(ks[15], D_MODEL),
        'post_mlp_norm': gain(ks[16], D_MODEL),
        'w_up': nrm(ks[17], (DEPTH, D_MODEL, D_FF), D_MODEL),
        'w_down': nrm(ks[18], (DEPTH, D_FF, D_MODEL), D_FF),
    }


def reference(x_prompt, x_sample, w_in, q_a_norm, w_q_b, kv_a_norm, w_kv_b, w_gk_f, b_gk_f, w_gk_b, b_gk_b, gla_norm, w_out, pre_mix_norm, post_mix_norm, pre_mlp_norm, post_mlp_norm, w_up, w_down):
    y_prompt = trunk(x_prompt, w_in, q_a_norm, w_q_b, kv_a_norm, w_kv_b, w_gk_f, b_gk_f, w_gk_b, b_gk_b, gla_norm, w_out, pre_mix_norm, post_mix_norm, pre_mlp_norm, post_mlp_norm, w_up, w_down)
    y_sample = trunk(x_sample, w_in, q_a_norm, w_q_b, kv_a_norm, w_kv_b, w_gk_f, b_gk_f, w_gk_b, b_gk_b, gla_norm, w_out, pre_mix_norm, post_mix_norm, pre_mlp_norm, post_mlp_norm, w_up, w_down)
    return (y_prompt, y_sample)
```

```python
import functools
import math

import jax
import jax.numpy as jnp
from jax import lax
from jax.experimental import pallas as pl
from jax.experimental.pallas import tpu as pltpu

F32 = jnp.float32
BF16 = jnp.bfloat16

D_MODEL = 2048
MLA_HEADS = 8
QK_NOPE = 128
QK_ROPE = 64
QK_DIM = QK_NOPE + QK_ROPE
V_HEAD = 128
Q_LORA = 512
KV_LORA = 512
ROPE_THETA = 10000.0
GLA_HEADS = 4
GLA_DK = 128
GLA_DV = 256
GLA_GATE_RANK = 16
GLA_GATE_NORM = 16.0
GLA_CHUNK = 64
D_FF = 4 * D_MODEL
EPS = 1e-6

MLA_WIDTH = MLA_HEADS * V_HEAD
GLA_QK_WIDTH = GLA_HEADS * GLA_DK
GLA_V_WIDTH = GLA_HEADS * GLA_DV
MAIN_WIDTH = Q_LORA + KV_LORA + 2 * GLA_QK_WIDTH + 2 * GLA_V_WIDTH
TAIL_WIDTH = 128
HALF_ROPE = QK_ROPE // 2

Q_PRESCALE = (QK_DIM ** -0.5) * math.log2(math.e)

VMEM_LIMIT = 56 * 1024 * 1024

NT_DIMS = (((1,), (1,)), ((), ()))


def _params(*sem):
    return pltpu.CompilerParams(dimension_semantics=sem, vmem_limit_bytes=VMEM_LIMIT)


def _rms(x, gain):
    return x * lax.rsqrt(jnp.mean(x * x, axis=-1, keepdims=True) + EPS) * gain


def _in_proj_kernel(x_ref, g_ref, wm_ref, wt_ref, om_ref, ot_ref, h_ref):
    @pl.when(pl.program_id(1) == 0)
    def _():
        h = _rms(x_ref[...], g_ref[...]).astype(BF16)
        h_ref[...] = h
        ot_ref[...] = jnp.dot(h, wt_ref[...], preferred_element_type=F32)

    om_ref[...] = jnp.dot(h_ref[...], wm_ref[...], preferred_element_type=F32).astype(BF16)


def _in_proj(x, gain, w_main, w_tail, *, tm, tn):
    T = x.shape[0]
    return pl.pallas_call(
        _in_proj_kernel,
        grid=(T // tm, MAIN_WIDTH // tn),
        in_specs=[
            pl.BlockSpec((tm, D_MODEL), lambda i, j: (i, 0)),
            pl.BlockSpec((1, D_MODEL), lambda i, j: (0, 0)),
            pl.BlockSpec((D_MODEL, tn), lambda i, j: (0, j)),
            pl.BlockSpec((D_MODEL, TAIL_WIDTH), lambda i, j: (0, 0)),
        ],
        out_specs=[
            pl.BlockSpec((tm, tn), lambda i, j: (i, j)),
            pl.BlockSpec((tm, TAIL_WIDTH), lambda i, j: (i, 0)),
        ],
        out_shape=[
            jax.ShapeDtypeStruct((T, MAIN_WIDTH), BF16),
            jax.ShapeDtypeStruct((T, TAIL_WIDTH), F32),
        ],
        scratch_shapes=[pltpu.VMEM((tm, D_MODEL), BF16)],
        compiler_params=_params("parallel", "arbitrary"),
        name="in_proj",
    )(x, gain, w_main, w_tail)


def _log_sigmoid(z):
    return jnp.minimum(z, 0.0) - jnp.log(1.0 + jnp.exp(-jnp.abs(z)))


def _mla_prep_kernel(c_ref, t_ref, qn_ref, kvn_ref, wqT_ref, wk_ref, wvT_ref,
                     wgf_ref, bgf_ref, wgb_ref, bgb_ref, cos_ref, sin_ref, cosT_ref, sinT_ref,
                     qT_ref, k_ref, vT_ref, lgf_ref, lgb_ref):
    c = c_ref[0].astype(F32)
    cq = _rms(c[:, :Q_LORA], qn_ref[...]).astype(BF16)
    ckv = _rms(c[:, Q_LORA:], kvn_ref[...]).astype(BF16)

    qT = lax.dot_general(wqT_ref[...], cq, NT_DIMS, preferred_element_type=F32)
    cosT = cosT_ref[...]
    sinT = sinT_ref[...]
    for h in range(MLA_HEADS):
        r0 = h * QK_DIM
        x1 = qT[r0 + QK_NOPE:r0 + QK_NOPE + HALF_ROPE]
        x2 = qT[r0 + QK_NOPE + HALF_ROPE:r0 + QK_DIM]
        qT_ref[0, h, 0:QK_NOPE, :] = (qT[r0:r0 + QK_NOPE] * Q_PRESCALE).astype(BF16)
        qT_ref[0, h, QK_NOPE:QK_NOPE + HALF_ROPE, :] = ((x1 * cosT - x2 * sinT) * Q_PRESCALE).astype(BF16)
        qT_ref[0, h, QK_NOPE + HALF_ROPE:QK_DIM, :] = ((x2 * cosT + x1 * sinT) * Q_PRESCALE).astype(BF16)

    t = t_ref[0]
    k1 = t[:, 0:HALF_ROPE]
    k2 = t[:, HALF_ROPE:QK_ROPE]
    cos = cos_ref[...]
    sin = sin_ref[...]
    k_pe = jnp.concatenate([k1 * cos - k2 * sin, k2 * cos + k1 * sin], axis=-1).astype(BF16)
    kn = jnp.dot(ckv, wk_ref[...], preferred_element_type=F32)
    for h in range(MLA_HEADS):
        k_ref[0, h, :, 0:QK_NOPE] = kn[:, h * QK_NOPE:(h + 1) * QK_NOPE].astype(BF16)
        k_ref[0, h, :, QK_NOPE:QK_DIM] = k_pe

    vT = lax.dot_general(wvT_ref[...], ckv, NT_DIMS, preferred_element_type=F32)
    for h in range(MLA_HEADS):
        vT_ref[0, h] = vT[h * V_HEAD:(h + 1) * V_HEAD].astype(BF16)

    gf = t[:, QK_ROPE:QK_ROPE + GLA_GATE_RANK]
    gb = t[:, QK_ROPE + GLA_GATE_RANK:QK_ROPE + 2 * GLA_GATE_RANK]
    zf = jnp.dot(gf, wgf_ref[...], preferred_element_type=F32, precision=lax.Precision.HIGHEST) + bgf_ref[...]
    zb = jnp.dot(gb, wgb_ref[...], preferred_element_type=F32, precision=lax.Precision.HIGHEST) + bgb_ref[...]
    lgf_ref[0] = _log_sigmoid(zf) * (1.0 / GLA_GATE_NORM)
    lgb_ref[0] = _log_sigmoid(zb) * (1.0 / GLA_GATE_NORM)


def _mla_prep(main, tail, q_norm, kv_norm, wqT, wk, wvT, wgf, bgf, wgb, bgb, cos, sin, cosT, sinT, *, tm):
    B, L, _ = main.shape
    const = lambda shape: pl.BlockSpec(shape, lambda b, i: (0,) * len(shape))
    return pl.pallas_call(
        _mla_prep_kernel,
        grid=(B, L // tm),
        in_specs=[
            pl.BlockSpec((1, tm, Q_LORA + KV_LORA), lambda b, i: (b, i, 0)),
            pl.BlockSpec((1, tm, TAIL_WIDTH), lambda b, i: (b, i, 0)),
            const((1, Q_LORA)),
            const((1, KV_LORA)),
            const((MLA_HEADS * QK_DIM, Q_LORA)),
            const((KV_LORA, MLA_HEADS * QK_NOPE)),
            const((MLA_HEADS * V_HEAD, KV_LORA)),
            const((GLA_GATE_RANK, GLA_QK_WIDTH)),
            const((1, GLA_QK_WIDTH)),
            const((GLA_GATE_RANK, GLA_QK_WIDTH)),
            const((1, GLA_QK_WIDTH)),
            pl.BlockSpec((tm, HALF_ROPE), lambda b, i: (i, 0)),
            pl.BlockSpec((tm, HALF_ROPE), lambda b, i: (i, 0)),
            pl.BlockSpec((HALF_ROPE, tm), lambda b, i: (0, i)),
            pl.BlockSpec((HALF_ROPE, tm), lambda b, i: (0, i)),
        ],
        out_specs=[
            pl.BlockSpec((1, MLA_HEADS, QK_DIM, tm), lambda b, i: (b, 0, 0, i)),
            pl.BlockSpec((1, MLA_HEADS, tm, QK_DIM), lambda b, i: (b, 0, i, 0)),
            pl.BlockSpec((1, MLA_HEADS, V_HEAD, tm), lambda b, i: (b, 0, 0, i)),
            pl.BlockSpec((1, tm, GLA_QK_WIDTH), lambda b, i: (b, i, 0)),
            pl.BlockSpec((1, tm, GLA_QK_WIDTH), lambda b, i: (b, i, 0)),
        ],
        out_shape=[
            jax.ShapeDtypeStruct((B, MLA_HEADS, QK_DIM, L), BF16),
            jax.ShapeDtypeStruct((B, MLA_HEADS, L, QK_DIM), BF16),
            jax.ShapeDtypeStruct((B, MLA_HEADS, V_HEAD, L), BF16),
            jax.ShapeDtypeStruct((B, L, GLA_QK_WIDTH), F32),
            jax.ShapeDtypeStruct((B, L, GLA_QK_WIDTH), F32),
        ],
        compiler_params=_params("parallel", "parallel"),
        name="mla_prep",
    )(main, tail, q_norm, kv_norm, wqT, wk, wvT, wgf, bgf, wgb, bgb, cos, sin, cosT, sinT)


def _attention_kernel(qT_ref, k_ref, vT_ref, o_ref, *, tk, nk):
    qT = qT_ref[0, 0]
    tq = qT.shape[1]

    def body(j, carry):
        m, l, acc = carry
        off = pl.multiple_of(j * tk, tk)
        s = jnp.dot(k_ref[0, 0, pl.ds(off, tk), :], qT, preferred_element_type=F32)
        m_new = jnp.maximum(m, jnp.max(s, axis=0, keepdims=True))
        alpha = jnp.exp2(m - m_new)
        p = jnp.exp2(s - m_new)
        l = alpha * l + jnp.sum(p, axis=0, keepdims=True)
        pv = jnp.dot(vT_ref[0, 0, :, pl.ds(off, tk)], p.astype(BF16), preferred_element_type=F32)
        return m_new, l, alpha * acc + pv

    m0 = jnp.full((1, tq), -jnp.inf, F32)
    l0 = jnp.zeros((1, tq), F32)
    acc0 = jnp.zeros((V_HEAD, tq), F32)
    _, l, acc = lax.fori_loop(0, nk, body, (m0, l0, acc0))
    o_ref[0] = (acc / l).T.astype(BF16)


def _attention(qT, k, vT, *, tq, tk):
    B, H, _, L = qT.shape
    return pl.pallas_call(
        functools.partial(_attention_kernel, tk=tk, nk=L // tk),
        grid=(B, H, L // tq),
        in_specs=[
            pl.BlockSpec((1, 1, QK_DIM, tq), lambda b, h, i: (b, h, 0, i)),
            pl.BlockSpec((1, 1, L, QK_DIM), lambda b, h, i: (b, h, 0, 0)),
            pl.BlockSpec((1, 1, V_HEAD, L), lambda b, h, i: (b, h, 0, 0)),
        ],
        out_specs=pl.BlockSpec((1, tq, V_HEAD), lambda b, h, i: (b, i, h)),
        out_shape=jax.ShapeDtypeStruct((B, L, H * V_HEAD), BF16),
        compiler_params=_params("parallel", "parallel", "arbitrary"),
        name="attention",
    )(qT, k, vT)


def _gla_kernel(*refs, reverse, final, nchunks):
    if final:
        q_ref, k_ref, v_ref, lg_ref, of_ref, og_ref, gn_ref, o_ref, st_ref = refs
    else:
        q_ref, k_ref, v_ref, lg_ref, o_ref, st_ref = refs
    C = GLA_CHUNK

    @pl.when(pl.program_id(1) == 0)
    def _():
        st_ref[...] = jnp.zeros_like(st_ref)

    row = lax.broadcasted_iota(jnp.int32, (C, C), 0)
    col = lax.broadcasted_iota(jnp.int32, (C, C), 1)
    keep = (col >= row) if reverse else (col <= row)
    tri = keep.astype(F32)
    last = 0 if reverse else C - 1

    def chunk(cc, _):
        c = (nchunks - 1 - cc) if reverse else cc
        rows = pl.ds(pl.multiple_of(c * C, C), C)
        g = lg_ref[0, rows, :]
        b = jnp.dot(tri, g, preferred_element_type=F32, precision=lax.Precision.HIGHEST)
        b_last = b[last:last + 1, :]
        q_dec = q_ref[0, rows, :].astype(F32) * (jnp.exp(b) * (GLA_DK ** -0.5))
        kf = k_ref[0, rows, :].astype(F32)
        k_inv = kf * jnp.exp(-b)
        k_tail = kf * jnp.exp(b_last - b)
        decay = jnp.exp(b_last)
        for h in range(GLA_HEADS):
            ks = slice(h * GLA_DK, (h + 1) * GLA_DK)
            vs = slice(h * GLA_DV, (h + 1) * GLA_DV)
            qd = q_dec[:, ks].astype(BF16)
            v = v_ref[0, rows, vs]
            a = lax.dot_general(qd, k_inv[:, ks].astype(BF16), NT_DIMS, preferred_element_type=F32)
            a = jnp.where(keep, a, 0.0).astype(BF16)
            st = st_ref[h]
            o = (jnp.dot(a, v, preferred_element_type=F32)
                 + lax.dot_general(qd, st.astype(BF16), NT_DIMS, preferred_element_type=F32))
            vT = v.astype(F32).T.astype(BF16)
            st_ref[h] = st * decay[:, ks] + jnp.dot(vT, k_tail[:, ks].astype(BF16),
                                                    preferred_element_type=F32)
            if final:
                o = o + of_ref[0, rows, vs]
                o = _rms(o, gn_ref[...])
                gate = og_ref[0, rows, vs].astype(F32)
                o = o * (gate * jax.nn.sigmoid(gate))
                o_ref[0, rows, vs] = o.astype(BF16)
            else:
                o_ref[0, rows, vs] = o
        return 0

    lax.fori_loop(0, nchunks, chunk, 0)


def _gla(main, lg, o_fwd, gla_norm, *, reverse, cb):
    B, L, _ = main.shape
    nb = L // cb
    final = o_fwd is not None
    blk = (lambda i: nb - 1 - i) if reverse else (lambda i: i)
    in_specs = [
        pl.BlockSpec((1, cb, GLA_QK_WIDTH), lambda b, i: (b, blk(i), 2)),
        pl.BlockSpec((1, cb, GLA_QK_WIDTH), lambda b, i: (b, blk(i), 3)),
        pl.BlockSpec((1, cb, GLA_V_WIDTH), lambda b, i: (b, blk(i), 2)),
        pl.BlockSpec((1, cb, GLA_QK_WIDTH), lambda b, i: (b, blk(i), 0)),
    ]
    args = [main, main, main, lg]
    if final:
        in_specs += [
            pl.BlockSpec((1, cb, GLA_V_WIDTH), lambda b, i: (b, blk(i), 0)),
            pl.BlockSpec((1, cb, GLA_V_WIDTH), lambda b, i: (b, blk(i), 3)),
            pl.BlockSpec((1, GLA_DV), lambda b, i: (0, 0)),
        ]
        args += [o_fwd, main, gla_norm]
    return pl.pallas_call(
        functools.partial(_gla_kernel, reverse=reverse, final=final, nchunks=cb // GLA_CHUNK),
        grid=(B, nb),
        in_specs=in_specs,
        out_specs=pl.BlockSpec((1, cb, GLA_V_WIDTH), lambda b, i: (b, blk(i), 0)),
        out_shape=jax.ShapeDtypeStruct((B, L, GLA_V_WIDTH), BF16 if final else F32),
        scratch_shapes=[pltpu.VMEM((GLA_HEADS, GLA_DV, GLA_DK), F32)],
        compiler_params=_params("parallel", "arbitrary"),
        name="gla_bwd" if reverse else "gla_fwd",
    )(*args)


def _out_proj_kernel(x_ref, mla_ref, gla_ref, wa_ref, wb_ref, g_ref, o_ref):
    mix = (jnp.dot(mla_ref[...], wa_ref[...], preferred_element_type=F32)
           + jnp.dot(gla_ref[...], wb_ref[...], preferred_element_type=F32))
    o_ref[...] = x_ref[...] + _rms(mix, g_ref[...])


def _out_proj(x, mla, gla, w_a, w_b, gain, *, tm):
    T = x.shape[0]
    return pl.pallas_call(
        _out_proj_kernel,
        grid=(T // tm,),
        in_specs=[
            pl.BlockSpec((tm, D_MODEL), lambda i: (i, 0)),
            pl.BlockSpec((tm, MLA_WIDTH), lambda i: (i, 0)),
            pl.BlockSpec((tm, GLA_V_WIDTH), lambda i: (i, 0)),
            pl.BlockSpec((MLA_WIDTH, D_MODEL), lambda i: (0, 0)),
            pl.BlockSpec((GLA_V_WIDTH, D_MODEL), lambda i: (0, 0)),
            pl.BlockSpec((1, D_MODEL), lambda i: (0, 0)),
        ],
        out_specs=pl.BlockSpec((tm, D_MODEL), lambda i: (i, 0)),
        out_shape=jax.ShapeDtypeStruct((T, D_MODEL), F32),
        compiler_params=_params("parallel"),
        name="out_proj",
    )(x, mla, gla, w_a, w_b, gain)


def _mlp_kernel(x_ref, g1_ref, wu_ref, wd_ref, g2_ref, o_ref, h_ref, acc_ref):
    j = pl.program_id(1)

    @pl.when(j == 0)
    def _():
        h_ref[...] = _rms(x_ref[...], g1_ref[...]).astype(BF16)
        acc_ref[...] = jnp.zeros_like(acc_ref)

    u = jnp.dot(h_ref[...], wu_ref[...], preferred_element_type=F32)
    u = jnp.square(jnp.maximum(u, 0.0)).astype(BF16)
    acc_ref[...] += jnp.dot(u, wd_ref[...], preferred_element_type=F32)

    @pl.when(j == pl.num_programs(1) - 1)
    def _():
        o_ref[...] = x_ref[...] + _rms(acc_ref[...], g2_ref[...])


def _mlp(x, g1, w_up, w_down, g2, *, tm, tf):
    T = x.shape[0]
    return pl.pallas_call(
        _mlp_kernel,
        grid=(T // tm, D_FF // tf),
        in_specs=[
            pl.BlockSpec((tm, D_MODEL), lambda i, j: (i, 0)),
            pl.BlockSpec((1, D_MODEL), lambda i, j: (0, 0)),
            pl.BlockSpec((D_MODEL, tf), lambda i, j: (0, j)),
            pl.BlockSpec((tf, D_MODEL), lambda i, j: (j, 0)),
            pl.BlockSpec((1, D_MODEL), lambda i, j: (0, 0)),
        ],
        out_specs=pl.BlockSpec((tm, D_MODEL), lambda i, j: (i, 0)),
        out_shape=jax.ShapeDtypeStruct((T, D_MODEL), F32),
        scratch_shapes=[pltpu.VMEM((tm, D_MODEL), BF16), pltpu.VMEM((tm, D_MODEL), F32)],
        compiler_params=_params("parallel", "arbitrary"),
        name="mlp",
    )(x, g1, w_up, w_down, g2)


def _pick(n, pref):
    t = min(n, pref)
    while n % t:
        t //= 2
    return t


def _rope_tables(L):
    inv = 1.0 / (ROPE_THETA ** (jnp.arange(0, QK_ROPE, 2, dtype=F32) / QK_ROPE))
    ang = jnp.arange(L, dtype=F32)[:, None] * inv[None, :]
    return jnp.cos(ang), jnp.sin(ang)


def _prepare_weights(w_in, q_a_norm, w_q_b, kv_a_norm, w_kv_b, w_gk_f, b_gk_f, w_gk_b, b_gk_b,
                     gla_norm, w_out, pre_mix_norm, post_mix_norm, pre_mlp_norm, post_mlp_norm,
                     w_up, w_down):
    s = [0]
    for width in (Q_LORA, KV_LORA, QK_ROPE, GLA_QK_WIDTH, GLA_QK_WIDTH, GLA_V_WIDTH,
                  GLA_GATE_RANK, GLA_GATE_RANK, GLA_V_WIDTH):
        s.append(s[-1] + width)
    col = lambda i: w_in[:, s[i]:s[i + 1]]
    w_main = jnp.concatenate([col(0), col(1), col(3), col(4), col(5), col(8)], axis=1).astype(BF16)
    pad = jnp.zeros((D_MODEL, TAIL_WIDTH - QK_ROPE - 2 * GLA_GATE_RANK), w_in.dtype)
    w_tail = jnp.concatenate([col(2), col(6), col(7), pad], axis=1).astype(BF16)
    w_kv = w_kv_b.reshape(KV_LORA, MLA_HEADS, QK_NOPE + V_HEAD)
    row = lambda v: v.reshape(1, -1).astype(F32)
    return dict(
        w_main=w_main, w_tail=w_tail,
        q_norm=row(q_a_norm), kv_norm=row(kv_a_norm),
        wqT=w_q_b.T.astype(BF16),
        wk=w_kv[:, :, :QK_NOPE].reshape(KV_LORA, MLA_HEADS * QK_NOPE).astype(BF16),
        wvT=w_kv[:, :, QK_NOPE:].reshape(KV_LORA, MLA_HEADS * V_HEAD).T.astype(BF16),
        wgf=w_gk_f.astype(F32), bgf=row(b_gk_f), wgb=w_gk_b.astype(F32), bgb=row(b_gk_b),
        gla_norm=row(gla_norm),
        w_out_a=w_out[:MLA_WIDTH].astype(BF16), w_out_b=w_out[MLA_WIDTH:].astype(BF16),
        pre_mix=row(pre_mix_norm), post_mix=row(post_mix_norm),
        pre_mlp=row(pre_mlp_norm), post_mlp=row(post_mlp_norm),
        w_up=w_up.astype(BF16), w_down=w_down.astype(BF16),
    )


def _layer(x, w):
    B, L, _ = x.shape
    T = B * L
    x2 = x.reshape(T, D_MODEL)
    cos, sin = _rope_tables(L)

    main, tail = _in_proj(x2, w["pre_mix"], w["w_main"], w["w_tail"], tm=_pick(T, 1024), tn=512)
    main = main.reshape(B, L, MAIN_WIDTH)
    tail = tail.reshape(B, L, TAIL_WIDTH)

    qT, k, vT, lgf, lgb = _mla_prep(
        main, tail, w["q_norm"], w["kv_norm"], w["wqT"], w["wk"], w["wvT"],
        w["wgf"], w["bgf"], w["wgb"], w["bgb"], cos, sin, cos.T, sin.T, tm=_pick(L, 512))
    mla = _attention(qT, k, vT, tq=_pick(L, 256), tk=_pick(L, 512))

    cb = _pick(L, 512)
    o_fwd = _gla(main, lgf, None, None, reverse=False, cb=cb)
    gla = _gla(main, lgb, o_fwd, w["gla_norm"], reverse=True, cb=cb)

    x2 = _out_proj(x2, mla.reshape(T, MLA_WIDTH), gla.reshape(T, GLA_V_WIDTH),
                   w["w_out_a"], w["w_out_b"], w["post_mix"], tm=_pick(T, 512))
    x2 = _mlp(x2, w["pre_mlp"], w["w_up"], w["w_down"], w["post_mlp"], tm=_pick(T, 512), tf=512)
    return x2.reshape(B, L, D_MODEL)


def kernel(x_prompt, x_sample, w_in, q_a_norm, w_q_b, kv_a_norm, w_kv_b, w_gk_f, b_gk_f, w_gk_b, b_gk_b, gla_norm, w_out, pre_mix_norm, post_mix_norm, pre_mlp_norm, post_mlp_norm, w_up, w_down):
    depth = w_in.shape[0]
    stacked = (w_in, q_a_norm, w_q_b, kv_a_norm, w_kv_b, w_gk_f, b_gk_f, w_gk_b, b_gk_b, gla_norm,
               w_out, pre_mix_norm, post_mix_norm, pre_mlp_norm, post_mlp_norm, w_up, w_down)
    layers = [_prepare_weights(*(t[l] for t in stacked)) for l in range(depth)]
    outs = []
    for x in (x_prompt, x_sample):
        for w in layers:
            x = _layer(x, w)
        outs.append(x)
    return tuple(outs)
```

```python
import functools
import math

import jax
import jax.numpy as jnp
from jax import lax
from jax.experimental import pallas as pl
from jax.experimental.pallas import tpu as pltpu

F32 = jnp.float32
BF16 = jnp.bfloat16

D_MODEL = 2048
MLA_HEADS = 8
QK_NOPE = 128
QK_ROPE = 64
QK_DIM = QK_NOPE + QK_ROPE
V_HEAD = 128
V_AUG = V_HEAD + 16
Q_LORA = 512
KV_LORA = 512
ROPE_THETA = 10000.0
GLA_HEADS = 4
GLA_DK = 128
GLA_DV = 256
GLA_GATE_RANK = 16
GLA_GATE_NORM = 16.0
GLA_CHUNK = 64
D_FF = 4 * D_MODEL
EPS = 1e-6

MLA_WIDTH = MLA_HEADS * V_HEAD
GLA_QK_WIDTH = GLA_HEADS * GLA_DK
GLA_V_WIDTH = GLA_HEADS * GLA_DV
MAIN_WIDTH = Q_LORA + KV_LORA + 2 * GLA_QK_WIDTH + 2 * GLA_V_WIDTH
TAIL_WIDTH = 128
HALF_ROPE = QK_ROPE // 2

Q_PRESCALE = (QK_DIM ** -0.5) * math.log2(math.e)

VMEM_LIMIT = 56 * 1024 * 1024

NT_DIMS = (((1,), (1,)), ((), ()))


def _params(*sem):
    return pltpu.CompilerParams(dimension_semantics=sem, vmem_limit_bytes=VMEM_LIMIT)


def _rms(x, gain):
    return x * lax.rsqrt(jnp.mean(x * x, axis=-1, keepdims=True) + EPS) * gain


def _in_proj_kernel(x_ref, g_ref, wm_ref, wt_ref, om_ref, ot_ref, h_ref):
    @pl.when(pl.program_id(1) == 0)
    def _():
        h = _rms(x_ref[...], g_ref[...]).astype(BF16)
        h_ref[...] = h
        ot_ref[...] = jnp.dot(h, wt_ref[...], preferred_element_type=F32)

    om_ref[...] = jnp.dot(h_ref[...], wm_ref[...], preferred_element_type=F32).astype(BF16)


def _in_proj(x, gain, w_main, w_tail, *, tm, tn):
    T = x.shape[0]
    return pl.pallas_call(
        _in_proj_kernel,
        grid=(T // tm, MAIN_WIDTH // tn),
        in_specs=[
            pl.BlockSpec((tm, D_MODEL), lambda i, j: (i, 0)),
            pl.BlockSpec((1, D_MODEL), lambda i, j: (0, 0)),
            pl.BlockSpec((D_MODEL, tn), lambda i, j: (0, j)),
            pl.BlockSpec((D_MODEL, TAIL_WIDTH), lambda i, j: (0, 0)),
        ],
        out_specs=[
            pl.BlockSpec((tm, tn), lambda i, j: (i, j)),
            pl.BlockSpec((tm, TAIL_WIDTH), lambda i, j: (i, 0)),
        ],
        out_shape=[
            jax.ShapeDtypeStruct((T, MAIN_WIDTH), BF16),
            jax.ShapeDtypeStruct((T, TAIL_WIDTH), F32),
        ],
        scratch_shapes=[pltpu.VMEM((tm, D_MODEL), BF16)],
        compiler_params=_params("parallel", "arbitrary"),
        name="in_proj",
    )(x, gain, w_main, w_tail)


def _log_sigmoid(z):
    return jnp.minimum(z, 0.0) - jnp.log(1.0 + jnp.exp(-jnp.abs(z)))


def _mla_prep_kernel(c_ref, t_ref, qn_ref, kvn_ref, wqT_ref, wk_ref, wvT_ref,
                     wgf_ref, bgf_ref, wgb_ref, bgb_ref, cos_ref, sin_ref, cosT_ref, sinT_ref,
                     qT_ref, k_ref, vT_ref, lgf_ref, lgb_ref):
    c = c_ref[0].astype(F32)
    cq = _rms(c[:, :Q_LORA], qn_ref[...]).astype(BF16)
    ckv = _rms(c[:, Q_LORA:], kvn_ref[...]).astype(BF16)

    qT = lax.dot_general(wqT_ref[...], cq, NT_DIMS, preferred_element_type=F32)
    cosT = cosT_ref[...]
    sinT = sinT_ref[...]
    for h in range(MLA_HEADS):
        r0 = h * QK_DIM
        x1 = qT[r0 + QK_NOPE:r0 + QK_NOPE + HALF_ROPE]
        x2 = qT[r0 + QK_NOPE + HALF_ROPE:r0 + QK_DIM]
        qT_ref[0, h, 0:QK_NOPE, :] = (qT[r0:r0 + QK_NOPE] * Q_PRESCALE).astype(BF16)
        qT_ref[0, h, QK_NOPE:QK_NOPE + HALF_ROPE, :] = ((x1 * cosT - x2 * sinT) * Q_PRESCALE).astype(BF16)
        qT_ref[0, h, QK_NOPE + HALF_ROPE:QK_DIM, :] = ((x2 * cosT + x1 * sinT) * Q_PRESCALE).astype(BF16)

    t = t_ref[0]
    k1 = t[:, 0:HALF_ROPE]
    k2 = t[:, HALF_ROPE:QK_ROPE]
    cos = cos_ref[...]
    sin = sin_ref[...]
    k_pe = jnp.concatenate([k1 * cos - k2 * sin, k2 * cos + k1 * sin], axis=-1).astype(BF16)
    kn = jnp.dot(ckv, wk_ref[...], preferred_element_type=F32)
    for h in range(MLA_HEADS):
        k_ref[0, h, :, 0:QK_NOPE] = kn[:, h * QK_NOPE:(h + 1) * QK_NOPE].astype(BF16)
        k_ref[0, h, :, QK_NOPE:QK_DIM] = k_pe

    vT = lax.dot_general(wvT_ref[...], ckv, NT_DIMS, preferred_element_type=F32)
    for h in range(MLA_HEADS):
        vT_ref[0, h, 0:V_HEAD, :] = vT[h * V_HEAD:(h + 1) * V_HEAD].astype(BF16)
        vT_ref[0, h, V_HEAD:V_AUG, :] = jnp.ones((V_AUG - V_HEAD, vT.shape[1]), BF16)

    gf = t[:, QK_ROPE:QK_ROPE + GLA_GATE_RANK]
    gb = t[:, QK_ROPE + GLA_GATE_RANK:QK_ROPE + 2 * GLA_GATE_RANK]
    zf = jnp.dot(gf, wgf_ref[...], preferred_element_type=F32, precision=lax.Precision.HIGHEST) + bgf_ref[...]
    zb = jnp.dot(gb, wgb_ref[...], preferred_element_type=F32, precision=lax.Precision.HIGHEST) + bgb_ref[...]
    lgf_ref[0] = _log_sigmoid(zf) * (1.0 / GLA_GATE_NORM)
    lgb_ref[0] = _log_sigmoid(zb) * (1.0 / GLA_GATE_NORM)


def _mla_prep(main, tail, q_norm, kv_norm, wqT, wk, wvT, wgf, bgf, wgb, bgb, cos, sin, cosT, sinT, *, tm):
    B, L, _ = main.shape
    const = lambda shape: pl.BlockSpec(shape, lambda b, i: (0,) * len(shape))
    return pl.pallas_call(
        _mla_prep_kernel,
        grid=(B, L // tm),
        in_specs=[
            pl.BlockSpec((1, tm, Q_LORA + KV_LORA), lambda b, i: (b, i, 0)),
            pl.BlockSpec((1, tm, TAIL_WIDTH), lambda b, i: (b, i, 0)),
            const((1, Q_LORA)),
            const((1, KV_LORA)),
            const((MLA_HEADS * QK_DIM, Q_LORA)),
            const((KV_LORA, MLA_HEADS * QK_NOPE)),
            const((MLA_HEADS * V_HEAD, KV_LORA)),
            const((GLA_GATE_RANK, GLA_QK_WIDTH)),
            const((1, GLA_QK_WIDTH)),
            const((GLA_GATE_RANK, GLA_QK_WIDTH)),
            const((1, GLA_QK_WIDTH)),
            pl.BlockSpec((tm, HALF_ROPE), lambda b, i: (i, 0)),
            pl.BlockSpec((tm, HALF_ROPE), lambda b, i: (i, 0)),
            pl.BlockSpec((HALF_ROPE, tm), lambda b, i: (0, i)),
            pl.BlockSpec((HALF_ROPE, tm), lambda b, i: (0, i)),
        ],
        out_specs=[
            pl.BlockSpec((1, MLA_HEADS, QK_DIM, tm), lambda b, i: (b, 0, 0, i)),
            pl.BlockSpec((1, MLA_HEADS, tm, QK_DIM), lambda b, i: (b, 0, i, 0)),
            pl.BlockSpec((1, MLA_HEADS, V_AUG, tm), lambda b, i: (b, 0, 0, i)),
            pl.BlockSpec((1, tm, GLA_QK_WIDTH), lambda b, i: (b, i, 0)),
            pl.BlockSpec((1, tm, GLA_QK_WIDTH), lambda b, i: (b, i, 0)),
        ],
        out_shape=[
            jax.ShapeDtypeStruct((B, MLA_HEADS, QK_DIM, L), BF16),
            jax.ShapeDtypeStruct((B, MLA_HEADS, L, QK_DIM), BF16),
            jax.ShapeDtypeStruct((B, MLA_HEADS, V_AUG, L), BF16),
            jax.ShapeDtypeStruct((B, L, GLA_QK_WIDTH), F32),
            jax.ShapeDtypeStruct((B, L, GLA_QK_WIDTH), F32),
        ],
        compiler_params=_params("parallel", "parallel"),
        name="mla_prep",
    )(main, tail, q_norm, kv_norm, wqT, wk, wvT, wgf, bgf, wgb, bgb, cos, sin, cosT, sinT)


def _attention_kernel(qT_ref, k_ref, vT_ref, o_ref, *scratch, tq, tk, nk, groups, nsteps):
    tb = groups * tq
    per_slot = lambda refs: (refs[:groups], refs[groups:])
    s_ref = per_slot(scratch[0:2 * groups])
    p_ref = per_slot(scratch[2 * groups:4 * groups])
    alpha_ref = per_slot(scratch[4 * groups:6 * groups])
    m_ref = scratch[6 * groups:7 * groups]
    acc_ref = scratch[7 * groups:8 * groups]

    def scores(t, g, slot):
        koff = pl.multiple_of((t % nk) * tk, tk)
        qoff = pl.multiple_of((t // nk) * tb + g * tq, tq)
        s_ref[slot][g][...] = jnp.dot(k_ref[0, 0, pl.ds(koff, tk), :], qT_ref[0, 0, :, pl.ds(qoff, tq)],
                                      preferred_element_type=F32)

    def softmax(t, g, slot):
        m_old = jnp.where(t % nk == 0, -jnp.inf, m_ref[g][...])
        m_new = jnp.maximum(m_old, jnp.max(s_ref[slot][g][...], axis=0, keepdims=True))
        p_ref[slot][g][...] = jnp.exp2(s_ref[slot][g][...] - m_new).astype(BF16)
        alpha_ref[slot][g][...] = jnp.exp2(m_old - m_new)
        m_ref[g][...] = m_new

    def pv(t, g, slot, emit):
        koff = pl.multiple_of((t % nk) * tk, tk)
        upd = jnp.dot(vT_ref[0, 0, :, pl.ds(koff, tk)], p_ref[slot][g][...], preferred_element_type=F32)
        acc = alpha_ref[slot][g][...] * acc_ref[g][...] + upd
        acc_ref[g][...] = acc
        if emit:
            qoff = pl.multiple_of((t // nk) * tb + g * tq, tq)
            o = acc[:V_HEAD] / acc[V_HEAD:V_HEAD + 1]
            o_ref[0, pl.ds(qoff, tq), :] = o.T.astype(BF16)

    for g in range(groups):
        p_ref[1][g][...] = jnp.ones((tk, tq), BF16)
        alpha_ref[1][g][...] = jnp.zeros((1, tq), F32)
        m_ref[g][...] = jnp.full((1, tq), -jnp.inf, F32)
        acc_ref[g][...] = jnp.zeros((V_AUG, tq), F32)
        scores(0, g, 0)

    def pair(i, _):
        t0 = 2 * i
        for g in range(groups):
            scores(t0 + 1, g, 1)
            pv(jnp.maximum(t0 - 1, 0), g, 1, True)
            softmax(t0, g, 0)
        for g in range(groups):
            scores(jnp.minimum(t0 + 2, nsteps - 1), g, 0)
            pv(t0, g, 0, False)
            softmax(t0 + 1, g, 1)
        return 0

    lax.fori_loop(0, nsteps // 2, pair, 0)
    for g in range(groups):
        pv(nsteps - 1, g, 1, True)


def _attention(qT, k, vT, *, tq, tk, groups):
    B, H, _, L = qT.shape
    nk = L // tk
    nsteps = (L // (tq * groups)) * nk
    assert nk % 2 == 0
    once = dict(pipeline_mode=pl.Buffered(1))
    return pl.pallas_call(
        functools.partial(_attention_kernel, tq=tq, tk=tk, nk=nk, groups=groups, nsteps=nsteps),
        grid=(B, H),
        in_specs=[
            pl.BlockSpec((1, 1, QK_DIM, L), lambda b, h: (b, h, 0, 0), **once),
            pl.BlockSpec((1, 1, L, QK_DIM), lambda b, h: (b, h, 0, 0), **once),
            pl.BlockSpec((1, 1, V_AUG, L), lambda b, h: (b, h, 0, 0), **once),
        ],
        out_specs=pl.BlockSpec((1, L, V_HEAD), lambda b, h: (b, 0, h)),
        out_shape=jax.ShapeDtypeStruct((B, L, H * V_HEAD), BF16),
        scratch_shapes=(
            [pltpu.VMEM((tk, tq), F32)] * (2 * groups)
            + [pltpu.VMEM((tk, tq), BF16)] * (2 * groups)
            + [pltpu.VMEM((1, tq), F32)] * (2 * groups)
            + [pltpu.VMEM((1, tq), F32)] * groups
            + [pltpu.VMEM((V_AUG, tq), F32)] * groups
        ),
        compiler_params=_params("parallel", "arbitrary"),
        name="attention",
    )(qT, k, vT)


def _gla_kernel(*refs, reverse, final, nchunks):
    if final:
        q_ref, k_ref, v_ref, lg_ref, of_ref, og_ref, gn_ref, o_ref, st_ref = refs
    else:
        q_ref, k_ref, v_ref, lg_ref, o_ref, st_ref = refs
    C = GLA_CHUNK

    @pl.when(pl.program_id(1) == 0)
    def _():
        st_ref[...] = jnp.zeros_like(st_ref)

    row = lax.broadcasted_iota(jnp.int32, (C, C), 0)
    col = lax.broadcasted_iota(jnp.int32, (C, C), 1)
    keep = (col >= row) if reverse else (col <= row)
    tri = keep.astype(F32)
    last = 0 if reverse else C - 1

    def chunk(cc, _):
        c = (nchunks - 1 - cc) if reverse else cc
        rows = pl.ds(pl.multiple_of(c * C, C), C)
        g = lg_ref[0, rows, :]
        b = jnp.dot(tri, g, preferred_element_type=F32, precision=lax.Precision.HIGHEST)
        b_last = b[last:last + 1, :]
        q_dec = q_ref[0, rows, :].astype(F32) * (jnp.exp(b) * (GLA_DK ** -0.5))
        kf = k_ref[0, rows, :].astype(F32)
        k_inv = kf * jnp.exp(-b)
        k_tail = kf * jnp.exp(b_last - b)
        decay = jnp.exp(b_last)
        for h in range(GLA_HEADS):
            ks = slice(h * GLA_DK, (h + 1) * GLA_DK)
            vs = slice(h * GLA_DV, (h + 1) * GLA_DV)
            qd = q_dec[:, ks].astype(BF16)
            v = v_ref[0, rows, vs]
            a = lax.dot_general(qd, k_inv[:, ks].astype(BF16), NT_DIMS, preferred_element_type=F32)
            a = jnp.where(keep, a, 0.0).astype(BF16)
            st = st_ref[h]
            o = (jnp.dot(a, v, preferred_element_type=F32)
                 + lax.dot_general(qd, st.astype(BF16), NT_DIMS, preferred_element_type=F32))
            vT = v.astype(F32).T.astype(BF16)
            st_ref[h] = st * decay[:, ks] + jnp.dot(vT, k_tail[:, ks].astype(BF16),
                                                    preferred_element_type=F32)
            if final:
                o = o + of_ref[0, rows, vs]
                o = _rms(o, gn_ref[...])
                gate = og_ref[0, rows, vs].astype(F32)
                o = o * (gate * jax.nn.sigmoid(gate))
                o_ref[0, rows, vs] = o.astype(BF16)
            else:
                o_ref[0, rows, vs] = o
        return 0

    lax.fori_loop(0, nchunks, chunk, 0)


def _gla(main, lg, o_fwd, gla_norm, *, reverse, cb):
    B, L, _ = main.shape
    nb = L // cb
    final = o_fwd is not None
    blk = (lambda i: nb - 1 - i) if reverse else (lambda i: i)
    in_specs = [
        pl.BlockSpec((1, cb, GLA_QK_WIDTH), lambda b, i: (b, blk(i), 2)),
        pl.BlockSpec((1, cb, GLA_QK_WIDTH), lambda b, i: (b, blk(i), 3)),
        pl.BlockSpec((1, cb, GLA_V_WIDTH), lambda b, i: (b, blk(i), 2)),
        pl.BlockSpec((1, cb, GLA_QK_WIDTH), lambda b, i: (b, blk(i), 0)),
    ]
    args = [main, main, main, lg]
    if final:
        in_specs += [
            pl.BlockSpec((1, cb, GLA_V_WIDTH), lambda b, i: (b, blk(i), 0)),
            pl.BlockSpec((1, cb, GLA_V_WIDTH), lambda b, i: (b, blk(i), 3)),
            pl.BlockSpec((1, GLA_DV), lambda b, i: (0, 0)),
        ]
        args += [o_fwd, main, gla_norm]
    return pl.pallas_call(
        functools.partial(_gla_kernel, reverse=reverse, final=final, nchunks=cb // GLA_CHUNK),
        grid=(B, nb),
        in_specs=in_specs,
        out_specs=pl.BlockSpec((1, cb, GLA_V_WIDTH), lambda b, i: (b, blk(i), 0)),
        out_shape=jax.ShapeDtypeStruct((B, L, GLA_V_WIDTH), BF16 if final else F32),
        scratch_shapes=[pltpu.VMEM((GLA_HEADS, GLA_DV, GLA_DK), F32)],
        compiler_params=_params("parallel", "arbitrary"),
        name="gla_bwd" if reverse else "gla_fwd",
    )(*args)


def _out_proj_kernel(x_ref, mla_ref, gla_ref, wa_ref, wb_ref, g_ref, o_ref):
    mix = (jnp.dot(mla_ref[...], wa_ref[...], preferred_element_type=F32)
           + jnp.dot(gla_ref[...], wb_ref[...], preferred_element_type=F32))
    o_ref[...] = x_ref[...] + _rms(mix, g_ref[...])


def _out_proj(x, mla, gla, w_a, w_b, gain, *, tm):
    T = x.shape[0]
    return pl.pallas_call(
        _out_proj_kernel,
        grid=(T // tm,),
        in_specs=[
            pl.BlockSpec((tm, D_MODEL), lambda i: (i, 0)),
            pl.BlockSpec((tm, MLA_WIDTH), lambda i: (i, 0)),
            pl.BlockSpec((tm, GLA_V_WIDTH), lambda i: (i, 0)),
            pl.BlockSpec((MLA_WIDTH, D_MODEL), lambda i: (0, 0)),
            pl.BlockSpec((GLA_V_WIDTH, D_MODEL), lambda i: (0, 0)),
            pl.BlockSpec((1, D_MODEL), lambda i: (0, 0)),
        ],
        out_specs=pl.BlockSpec((tm, D_MODEL), lambda i: (i, 0)),
        out_shape=jax.ShapeDtypeStruct((T, D_MODEL), F32),
        compiler_params=_params("parallel"),
        name="out_proj",
    )(x, mla, gla, w_a, w_b, gain)


def _mlp_kernel(x_ref, g1_ref, wu_ref, wd_ref, g2_ref, o_ref, h_ref, acc_ref):
    j = pl.program_id(1)

    @pl.when(j == 0)
    def _():
        h_ref[...] = _rms(x_ref[...], g1_ref[...]).astype(BF16)
        acc_ref[...] = jnp.zeros_like(acc_ref)

    u = jnp.dot(h_ref[...], wu_ref[...], preferred_element_type=F32)
    u = jnp.square(jnp.maximum(u, 0.0)).astype(BF16)
    acc_ref[...] += jnp.dot(u, wd_ref[...], preferred_element_type=F32)

    @pl.when(j == pl.num_programs(1) - 1)
    def _():
        o_ref[...] = x_ref[...] + _rms(acc_ref[...], g2_ref[...])


def _mlp(x, g1, w_up, w_down, g2, *, tm, tf):
    T = x.shape[0]
    return pl.pallas_call(
        _mlp_kernel,
        grid=(T // tm, D_FF // tf),
        in_specs=[
            pl.BlockSpec((tm, D_MODEL), lambda i, j: (i, 0)),
            pl.BlockSpec((1, D_MODEL), lambda i, j: (0, 0)),
            pl.BlockSpec((D_MODEL, tf), lambda i, j: (0, j)),
            pl.BlockSpec((tf, D_MODEL), lambda i, j: (j, 0)),
            pl.BlockSpec((1, D_MODEL), lambda i, j: (0, 0)),
        ],
        out_specs=pl.BlockSpec((tm, D_MODEL), lambda i, j: (i, 0)),
        out_shape=jax.ShapeDtypeStruct((T, D_MODEL), F32),
        scratch_shapes=[pltpu.VMEM((tm, D_MODEL), BF16), pltpu.VMEM((tm, D_MODEL), F32)],
        compiler_params=_params("parallel", "arbitrary"),
        name="mlp",
    )(x, g1, w_up, w_down, g2)


def _pick(n, pref):
    t = min(n, pref)
    while n % t:
        t //= 2
    return t


def _rope_tables(L):
    inv = 1.0 / (ROPE_THETA ** (jnp.arange(0, QK_ROPE, 2, dtype=F32) / QK_ROPE))
    ang = jnp.arange(L, dtype=F32)[:, None] * inv[None, :]
    return jnp.cos(ang), jnp.sin(ang)


def _prepare_weights(w_in, q_a_norm, w_q_b, kv_a_norm, w_kv_b, w_gk_f, b_gk_f, w_gk_b, b_gk_b,
                     gla_norm, w_out, pre_mix_norm, post_mix_norm, pre_mlp_norm, post_mlp_norm,
                     w_up, w_down):
    s = [0]
    for width in (Q_LORA, KV_LORA, QK_ROPE, GLA_QK_WIDTH, GLA_QK_WIDTH, GLA_V_WIDTH,
                  GLA_GATE_RANK, GLA_GATE_RANK, GLA_V_WIDTH):
        s.append(s[-1] + width)
    col = lambda i: w_in[:, s[i]:s[i + 1]]
    w_main = jnp.concatenate([col(0), col(1), col(3), col(4), col(5), col(8)], axis=1).astype(BF16)
    pad = jnp.zeros((D_MODEL, TAIL_WIDTH - QK_ROPE - 2 * GLA_GATE_RANK), w_in.dtype)
    w_tail = jnp.concatenate([col(2), col(6), col(7), pad], axis=1).astype(BF16)
    w_kv = w_kv_b.reshape(KV_LORA, MLA_HEADS, QK_NOPE + V_HEAD)
    row = lambda v: v.reshape(1, -1).astype(F32)
    return dict(
        w_main=w_main, w_tail=w_tail,
        q_norm=row(q_a_norm), kv_norm=row(kv_a_norm),
        wqT=w_q_b.T.astype(BF16),
        wk=w_kv[:, :, :QK_NOPE].reshape(KV_LORA, MLA_HEADS * QK_NOPE).astype(BF16),
        wvT=w_kv[:, :, QK_NOPE:].reshape(KV_LORA, MLA_HEADS * V_HEAD).T.astype(BF16),
        wgf=w_gk_f.astype(F32), bgf=row(b_gk_f), wgb=w_gk_b.astype(F32), bgb=row(b_gk_b),
        gla_norm=row(gla_norm),
        w_out_a=w_out[:MLA_WIDTH].astype(BF16), w_out_b=w_out[MLA_WIDTH:].astype(BF16),
        pre_mix=row(pre_mix_norm), post_mix=row(post_mix_norm),
        pre_mlp=row(pre_mlp_norm), post_mlp=row(post_mlp_norm),
        w_up=w_up.astype(BF16), w_down=w_down.astype(BF16),
    )


def _layer(x, w):
    B, L, _ = x.shape
    T = B * L
    x2 = x.reshape(T, D_MODEL)
    cos, sin = _rope_tables(L)

    main, tail = _in_proj(x2, w["pre_mix"], w["w_main"], w["w_tail"], tm=_pick(T, 1024), tn=512)
    main = main.reshape(B, L, MAIN_WIDTH)
    tail = tail.reshape(B, L, TAIL_WIDTH)

    qT, k, vT, lgf, lgb = _mla_prep(
        main, tail, w["q_norm"], w["kv_norm"], w["wqT"], w["wk"], w["wvT"],
        w["wgf"], w["bgf"], w["wgb"], w["bgb"], cos, sin, cos.T, sin.T, tm=_pick(L, 512))
    mla = _attention(qT, k, vT, tq=256, tk=_pick(L, 512), groups=_pick(L // 256, 8))

    cb = _pick(L, 512)
    o_fwd = _gla(main, lgf, None, None, reverse=False, cb=cb)
    gla = _gla(main, lgb, o_fwd, w["gla_norm"], reverse=True, cb=cb)

    x2 = _out_proj(x2, mla.reshape(T, MLA_WIDTH), gla.reshape(T, GLA_V_WIDTH),
                   w["w_out_a"], w["w_out_b"], w["post_mix"], tm=_pick(T, 512))
    x2 = _mlp(x2, w["pre_mlp"], w["w_up"], w["w_down"], w["post_mlp"], tm=_pick(T, 512), tf=512)
    return x2.reshape(B, L, D_MODEL)


def kernel(x_prompt, x_sample, w_in, q_a_norm, w_q_b, kv_a_norm, w_kv_b, w_gk_f, b_gk_f, w_gk_b, b_gk_b, gla_norm, w_out, pre_mix_norm, post_mix_norm, pre_mlp_norm, post_mlp_norm, w_up, w_down):
    depth = w_in.shape[0]
    stacked = (w_in, q_a_norm, w_q_b, kv_a_norm, w_kv_b, w_gk_f, b_gk_f, w_gk_b, b_gk_b, gla_norm,
               w_out, pre_mix_norm, post_mix_norm, pre_mlp_norm, post_mlp_norm, w_up, w_down)
    layers = [_prepare_weights(*(t[l] for t in stacked)) for l in range(depth)]
    outs = []
    for x in (x_prompt, x_sample):
        for w in layers:
            x = _layer(x, w)
        outs.append(x)
    return tuple(outs)
```

```python
import functools
import math

import jax
import jax.numpy as jnp
from jax import lax
from jax.experimental import pallas as pl
from jax.experimental.pallas import tpu as pltpu

F32 = jnp.float32
BF16 = jnp.bfloat16

D_MODEL = 2048
MLA_HEADS = 8
QK_NOPE = 128
QK_ROPE = 64
QK_DIM = QK_NOPE + QK_ROPE
V_HEAD = 128
V_AUG = V_HEAD + 16
Q_LORA = 512
KV_LORA = 512
ROPE_THETA = 10000.0
GLA_HEADS = 4
GLA_DK = 128
GLA_DV = 256
GLA_GATE_RANK = 16
GLA_GATE_NORM = 16.0
GLA_CHUNK = 64
D_FF = 4 * D_MODEL
EPS = 1e-6

MLA_WIDTH = MLA_HEADS * V_HEAD
GLA_QK_WIDTH = GLA_HEADS * GLA_DK
GLA_V_WIDTH = GLA_HEADS * GLA_DV
MAIN_WIDTH = Q_LORA + KV_LORA + 2 * GLA_QK_WIDTH + 2 * GLA_V_WIDTH
TAIL_WIDTH = 128
HALF_ROPE = QK_ROPE // 2

Q_PRESCALE = (QK_DIM ** -0.5) * math.log2(math.e)

VMEM_LIMIT = 56 * 1024 * 1024
LANE = 128
MXU_WIDTH = 256

NT_DIMS = (((1,), (1,)), ((), ()))


def _params(*sem):
    return pltpu.CompilerParams(dimension_semantics=sem, vmem_limit_bytes=VMEM_LIMIT)


def _rms(x, gain):
    return x * lax.rsqrt(jnp.mean(x * x, axis=-1, keepdims=True) + EPS) * gain


def _in_proj_kernel(x_ref, g_ref, wm_ref, wt_ref, om_ref, ot_ref, h_ref):
    @pl.when(pl.program_id(1) == 0)
    def _():
        h = _rms(x_ref[...], g_ref[...]).astype(BF16)
        h_ref[...] = h
        ot_ref[...] = jnp.dot(h, wt_ref[...], preferred_element_type=F32)

    om_ref[...] = jnp.dot(h_ref[...], wm_ref[...], preferred_element_type=F32).astype(BF16)


def _in_proj(x, gain, w_main, w_tail, *, tm, tn):
    T = x.shape[0]
    return pl.pallas_call(
        _in_proj_kernel,
        grid=(T // tm, MAIN_WIDTH // tn),
        in_specs=[
            pl.BlockSpec((tm, D_MODEL), lambda i, j: (i, 0)),
            pl.BlockSpec((1, D_MODEL), lambda i, j: (0, 0)),
            pl.BlockSpec((D_MODEL, tn), lambda i, j: (0, j)),
            pl.BlockSpec((D_MODEL, TAIL_WIDTH), lambda i, j: (0, 0)),
        ],
        out_specs=[
            pl.BlockSpec((tm, tn), lambda i, j: (i, j)),
            pl.BlockSpec((tm, TAIL_WIDTH), lambda i, j: (i, 0)),
        ],
        out_shape=[
            jax.ShapeDtypeStruct((T, MAIN_WIDTH), BF16),
            jax.ShapeDtypeStruct((T, TAIL_WIDTH), F32),
        ],
        scratch_shapes=[pltpu.VMEM((tm, D_MODEL), BF16)],
        compiler_params=_params("parallel", "arbitrary"),
        name="in_proj",
    )(x, gain, w_main, w_tail)


def _log_sigmoid(z):
    return jnp.minimum(z, 0.0) - jnp.log(1.0 + jnp.exp(-jnp.abs(z)))


def _mla_prep_kernel(c_ref, t_ref, qn_ref, kvn_ref, wqT_ref, wk_ref, wvT_ref,
                     wgf_ref, bgf_ref, wgb_ref, bgb_ref, cos_ref, sin_ref, cosT_ref, sinT_ref,
                     qT_ref, k_ref, vT_ref, lgf_ref, lgb_ref):
    c = c_ref[0].astype(F32)
    cq = _rms(c[:, :Q_LORA], qn_ref[...]).astype(BF16)
    ckv = _rms(c[:, Q_LORA:], kvn_ref[...]).astype(BF16)

    qT = lax.dot_general(wqT_ref[...], cq, NT_DIMS, preferred_element_type=F32)
    cosT = cosT_ref[...]
    sinT = sinT_ref[...]
    for h in range(MLA_HEADS):
        r0 = h * QK_DIM
        x1 = qT[r0 + QK_NOPE:r0 + QK_NOPE + HALF_ROPE]
        x2 = qT[r0 + QK_NOPE + HALF_ROPE:r0 + QK_DIM]
        qT_ref[0, h, 0:QK_NOPE, :] = (qT[r0:r0 + QK_NOPE] * Q_PRESCALE).astype(BF16)
        qT_ref[0, h, QK_NOPE:QK_NOPE + HALF_ROPE, :] = ((x1 * cosT - x2 * sinT) * Q_PRESCALE).astype(BF16)
        qT_ref[0, h, QK_NOPE + HALF_ROPE:QK_DIM, :] = ((x2 * cosT + x1 * sinT) * Q_PRESCALE).astype(BF16)

    t = t_ref[0]
    k1 = t[:, 0:HALF_ROPE]
    k2 = t[:, HALF_ROPE:QK_ROPE]
    cos = cos_ref[...]
    sin = sin_ref[...]
    k_pe = jnp.concatenate([k1 * cos - k2 * sin, k2 * cos + k1 * sin], axis=-1).astype(BF16)
    kn = jnp.dot(ckv, wk_ref[...], preferred_element_type=F32)
    for h in range(MLA_HEADS):
        k_ref[0, h, :, 0:QK_NOPE] = kn[:, h * QK_NOPE:(h + 1) * QK_NOPE].astype(BF16)
        k_ref[0, h, :, QK_NOPE:QK_DIM] = k_pe

    vT = lax.dot_general(wvT_ref[...], ckv, NT_DIMS, preferred_element_type=F32)
    for h in range(MLA_HEADS):
        vT_ref[0, h, 0:V_HEAD, :] = vT[h * V_HEAD:(h + 1) * V_HEAD].astype(BF16)
        vT_ref[0, h, V_HEAD:V_AUG, :] = jnp.ones((V_AUG - V_HEAD, vT.shape[1]), BF16)

    gf = t[:, QK_ROPE:QK_ROPE + GLA_GATE_RANK]
    gb = t[:, QK_ROPE + GLA_GATE_RANK:QK_ROPE + 2 * GLA_GATE_RANK]
    zf = jnp.dot(gf, wgf_ref[...], preferred_element_type=F32, precision=lax.Precision.HIGHEST) + bgf_ref[...]
    zb = jnp.dot(gb, wgb_ref[...], preferred_element_type=F32, precision=lax.Precision.HIGHEST) + bgb_ref[...]
    lgf_ref[0] = _log_sigmoid(zf) * (1.0 / GLA_GATE_NORM)
    lgb_ref[0] = _log_sigmoid(zb) * (1.0 / GLA_GATE_NORM)


def _mla_prep(main, tail, q_norm, kv_norm, wqT, wk, wvT, wgf, bgf, wgb, bgb, cos, sin, cosT, sinT, *, tm):
    B, L, _ = main.shape
    const = lambda shape: pl.BlockSpec(shape, lambda b, i: (0,) * len(shape))
    return pl.pallas_call(
        _mla_prep_kernel,
        grid=(B, L // tm),
        in_specs=[
            pl.BlockSpec((1, tm, Q_LORA + KV_LORA), lambda b, i: (b, i, 0)),
            pl.BlockSpec((1, tm, TAIL_WIDTH), lambda b, i: (b, i, 0)),
            const((1, Q_LORA)),
            const((1, KV_LORA)),
            const((MLA_HEADS * QK_DIM, Q_LORA)),
            const((KV_LORA, MLA_HEADS * QK_NOPE)),
            const((MLA_HEADS * V_HEAD, KV_LORA)),
            const((GLA_GATE_RANK, GLA_QK_WIDTH)),
            const((1, GLA_QK_WIDTH)),
            const((GLA_GATE_RANK, GLA_QK_WIDTH)),
            const((1, GLA_QK_WIDTH)),
            pl.BlockSpec((tm, HALF_ROPE), lambda b, i: (i, 0)),
            pl.BlockSpec((tm, HALF_ROPE), lambda b, i: (i, 0)),
            pl.BlockSpec((HALF_ROPE, tm), lambda b, i: (0, i)),
            pl.BlockSpec((HALF_ROPE, tm), lambda b, i: (0, i)),
        ],
        out_specs=[
            pl.BlockSpec((1, MLA_HEADS, QK_DIM, tm), lambda b, i: (b, 0, 0, i)),
            pl.BlockSpec((1, MLA_HEADS, tm, QK_DIM), lambda b, i: (b, 0, i, 0)),
            pl.BlockSpec((1, MLA_HEADS, V_AUG, tm), lambda b, i: (b, 0, 0, i)),
            pl.BlockSpec((1, tm, GLA_QK_WIDTH), lambda b, i: (b, i, 0)),
            pl.BlockSpec((1, tm, GLA_QK_WIDTH), lambda b, i: (b, i, 0)),
        ],
        out_shape=[
            jax.ShapeDtypeStruct((B, MLA_HEADS, QK_DIM, L), BF16),
            jax.ShapeDtypeStruct((B, MLA_HEADS, L, QK_DIM), BF16),
            jax.ShapeDtypeStruct((B, MLA_HEADS, V_AUG, L), BF16),
            jax.ShapeDtypeStruct((B, L, GLA_QK_WIDTH), F32),
            jax.ShapeDtypeStruct((B, L, GLA_QK_WIDTH), F32),
        ],
        compiler_params=_params("parallel", "parallel"),
        name="mla_prep",
    )(main, tail, q_norm, kv_norm, wqT, wk, wvT, wgf, bgf, wgb, bgb, cos, sin, cosT, sinT)


def _attention_kernel(qT_ref, k_ref, vT_ref, o_ref, *scratch, tq, tk, nk, groups, nsteps):
    tb = groups * tq
    per_slot = lambda refs: (refs[:groups], refs[groups:])
    s_ref = per_slot(scratch[0:2 * groups])
    p_ref = per_slot(scratch[2 * groups:4 * groups])
    alpha_ref = per_slot(scratch[4 * groups:6 * groups])
    m_ref = scratch[6 * groups:7 * groups]
    acc_ref = scratch[7 * groups:8 * groups]
    bmax_ref = per_slot(scratch[8 * groups:10 * groups])

    def scores(t, g, slot):
        koff = pl.multiple_of((t % nk) * tk, tk)
        qoff = pl.multiple_of((t // nk) * tb + g * tq, tq)
        s = jnp.dot(k_ref[0, 0, pl.ds(koff, tk), :], qT_ref[0, 0, :, pl.ds(qoff, tq)],
                    preferred_element_type=F32)
        s_ref[slot][g][...] = s
        bmax_ref[slot][g][...] = jnp.max(s, axis=0, keepdims=True)

    def softmax(t, g, slot):
        m_old = jnp.where(t % nk == 0, -jnp.inf, m_ref[g][...])
        m_new = jnp.maximum(m_old, bmax_ref[slot][g][...])
        p_ref[slot][g][...] = jnp.exp2(s_ref[slot][g][...] - m_new).astype(BF16)
        alpha_ref[slot][g][...] = jnp.exp2(m_old - m_new)
        m_ref[g][...] = m_new

    def pv(t, g, slot, emit):
        koff = pl.multiple_of((t % nk) * tk, tk)
        upd = jnp.dot(vT_ref[0, 0, :, pl.ds(koff, tk)], p_ref[slot][g][...], preferred_element_type=F32)
        acc = alpha_ref[slot][g][...] * acc_ref[g][...] + upd
        acc_ref[g][...] = acc
        if emit:
            qoff = pl.multiple_of((t // nk) * tb + g * tq, tq)
            o = acc[:V_HEAD] / acc[V_HEAD:V_HEAD + 1]
            o_ref[0, pl.ds(qoff, tq), :] = o.T.astype(BF16)

    for g in range(groups):
        p_ref[1][g][...] = jnp.ones((tk, tq), BF16)
        alpha_ref[1][g][...] = jnp.zeros((1, tq), F32)
        m_ref[g][...] = jnp.full((1, tq), -jnp.inf, F32)
        acc_ref[g][...] = jnp.zeros((V_AUG, tq), F32)
        scores(0, g, 0)

    def pair(i, _):
        t0 = 2 * i
        for g in range(groups):
            scores(t0 + 1, g, 1)
            pv(jnp.maximum(t0 - 1, 0), g, 1, True)
            softmax(t0, g, 0)
        for g in range(groups):
            scores(jnp.minimum(t0 + 2, nsteps - 1), g, 0)
            pv(t0, g, 0, False)
            softmax(t0 + 1, g, 1)
        return 0

    lax.fori_loop(0, nsteps // 2, pair, 0)
    for g in range(groups):
        pv(nsteps - 1, g, 1, True)


def _attention(qT, k, vT, *, tq, tk, groups):
    B, H, _, L = qT.shape
    nk = L // tk
    nsteps = (L // (tq * groups)) * nk
    assert nk % 2 == 0
    lanes = lambda n: -(-n // LANE) * LANE
    in_bytes = 2 * (QK_DIM * L + L * lanes(QK_DIM) + V_AUG * L)
    out_bytes = 2 * 2 * L * V_HEAD
    scratch_bytes = groups * (2 * tk * tq * (4 + 2) + V_AUG * tq * 4)
    fits = 2 * in_bytes + out_bytes + scratch_bytes <= (VMEM_LIMIT * 3) // 4
    mode = {} if fits else dict(pipeline_mode=pl.Buffered(1))
    return pl.pallas_call(
        functools.partial(_attention_kernel, tq=tq, tk=tk, nk=nk, groups=groups, nsteps=nsteps),
        grid=(B, H),
        in_specs=[
            pl.BlockSpec((1, 1, QK_DIM, L), lambda b, h: (b, h, 0, 0), **mode),
            pl.BlockSpec((1, 1, L, QK_DIM), lambda b, h: (b, h, 0, 0), **mode),
            pl.BlockSpec((1, 1, V_AUG, L), lambda b, h: (b, h, 0, 0), **mode),
        ],
        out_specs=pl.BlockSpec((1, L, V_HEAD), lambda b, h: (b, 0, h)),
        out_shape=jax.ShapeDtypeStruct((B, L, H * V_HEAD), BF16),
        scratch_shapes=(
            [pltpu.VMEM((tk, tq), F32)] * (2 * groups)
            + [pltpu.VMEM((tk, tq), BF16)] * (2 * groups)
            + [pltpu.VMEM((1, tq), F32)] * (2 * groups)
            + [pltpu.VMEM((1, tq), F32)] * groups
            + [pltpu.VMEM((V_AUG, tq), F32)] * groups
            + [pltpu.VMEM((1, tq), F32)] * (2 * groups)
        ),
        compiler_params=_params("parallel", "arbitrary"),
        name="attention",
    )(qT, k, vT)


def _gla_kernel(*refs, reverse, final, nchunks):
    if final:
        q_ref, k_ref, v_ref, lg_ref, of_ref, og_ref, gn_ref, o_ref, st_ref = refs
    else:
        q_ref, k_ref, v_ref, lg_ref, o_ref, st_ref = refs
    C = GLA_CHUNK

    @pl.when(pl.program_id(1) == 0)
    def _():
        st_ref[...] = jnp.zeros_like(st_ref)

    row = lax.broadcasted_iota(jnp.int32, (C, C), 0)
    col = lax.broadcasted_iota(jnp.int32, (C, C), 1)
    keep = (col >= row) if reverse else (col <= row)
    tri = keep.astype(BF16)
    last = 0 if reverse else C - 1
    order = [(nchunks - 1 - cc) if reverse else cc for cc in range(nchunks)]

    def gates(c):
        rows = slice(c * C, (c + 1) * C)
        g = lg_ref[0, rows, :]
        g_hi = g.astype(BF16)
        r1 = g - g_hi.astype(F32)
        g_mid = r1.astype(BF16)
        g_lo = (r1 - g_mid.astype(F32)).astype(BF16)
        b = (jnp.dot(tri, g_hi, preferred_element_type=F32)
             + jnp.dot(tri, g_mid, preferred_element_type=F32)
             + jnp.dot(tri, g_lo, preferred_element_type=F32))
        b_last = b[last:last + 1, :]
        kf = k_ref[0, rows, :].astype(F32)
        return dict(
            rows=rows,
            q_dec=(q_ref[0, rows, :].astype(F32) * (jnp.exp(b) * (GLA_DK ** -0.5))).astype(BF16),
            k_inv=(kf * jnp.exp(-b)).astype(BF16),
            k_tail=(kf * jnp.exp(b_last - b)).astype(BF16),
            decay=jnp.exp(b_last),
        )

    def products(ch):
        a, u = [], []
        for h in range(GLA_HEADS):
            ks = slice(h * GLA_DK, (h + 1) * GLA_DK)
            v = v_ref[0, ch["rows"], h * GLA_DV:(h + 1) * GLA_DV]
            a.append(lax.dot_general(ch["q_dec"][:, ks], ch["k_inv"][:, ks], NT_DIMS,
                                     preferred_element_type=F32))
            vT = v.astype(F32).T.astype(BF16)
            u.append(jnp.dot(vT, ch["k_tail"][:, ks], preferred_element_type=F32))
        ch["a"], ch["u"] = a, u

    def outputs(ch):
        rows = ch["rows"]
        for h in range(GLA_HEADS):
            ks = slice(h * GLA_DK, (h + 1) * GLA_DK)
            vs = slice(h * GLA_DV, (h + 1) * GLA_DV)
            a = jnp.where(keep, ch["a"][h], 0.0).astype(BF16)
            st = st_ref[h]
            o = (jnp.dot(a, v_ref[0, rows, vs], preferred_element_type=F32)
                 + lax.dot_general(ch["q_dec"][:, ks], st.astype(BF16), NT_DIMS,
                                   preferred_element_type=F32))
            st_ref[h] = st * ch["decay"][:, ks] + ch["u"][h]
            if final:
                o = o + of_ref[0, rows, vs]
                o = _rms(o, gn_ref[...])
                gate = og_ref[0, rows, vs].astype(F32)
                o = o * (gate * jax.nn.sigmoid(gate))
                o_ref[0, rows, vs] = o.astype(BF16)
            else:
                o_ref[0, rows, vs] = o

    chunks = {}
    for i in range(-2, nchunks):
        if i + 2 < nchunks:
            chunks[i + 2] = gates(order[i + 2])
        if 0 <= i + 1 < nchunks:
            products(chunks[i + 1])
        if i >= 0:
            outputs(chunks.pop(i))


def _gla(main, lg, o_fwd, gla_norm, *, reverse, cb):
    B, L, _ = main.shape
    nb = L // cb
    final = o_fwd is not None
    blk = (lambda i: nb - 1 - i) if reverse else (lambda i: i)
    in_specs = [
        pl.BlockSpec((1, cb, GLA_QK_WIDTH), lambda b, i: (b, blk(i), 2)),
        pl.BlockSpec((1, cb, GLA_QK_WIDTH), lambda b, i: (b, blk(i), 3)),
        pl.BlockSpec((1, cb, GLA_V_WIDTH), lambda b, i: (b, blk(i), 2)),
        pl.BlockSpec((1, cb, GLA_QK_WIDTH), lambda b, i: (b, blk(i), 0)),
    ]
    args = [main, main, main, lg]
    if final:
        in_specs += [
            pl.BlockSpec((1, cb, GLA_V_WIDTH), lambda b, i: (b, blk(i), 0)),
            pl.BlockSpec((1, cb, GLA_V_WIDTH), lambda b, i: (b, blk(i), 3)),
            pl.BlockSpec((1, GLA_DV), lambda b, i: (0, 0)),
        ]
        args += [o_fwd, main, gla_norm]
    return pl.pallas_call(
        functools.partial(_gla_kernel, reverse=reverse, final=final, nchunks=cb // GLA_CHUNK),
        grid=(B, nb),
        in_specs=in_specs,
        out_specs=pl.BlockSpec((1, cb, GLA_V_WIDTH), lambda b, i: (b, blk(i), 0)),
        out_shape=jax.ShapeDtypeStruct((B, L, GLA_V_WIDTH), BF16 if final else F32),
        scratch_shapes=[pltpu.VMEM((GLA_HEADS, GLA_DV, GLA_DK), F32)],
        compiler_params=_params("parallel", "arbitrary"),
        name="gla_bwd" if reverse else "gla_fwd",
    )(*args)


def _out_proj_kernel(x_ref, mla_ref, gla_ref, wa_ref, wb_ref, g_ref, o_ref):
    mix = (jnp.dot(mla_ref[...], wa_ref[...], preferred_element_type=F32)
           + jnp.dot(gla_ref[...], wb_ref[...], preferred_element_type=F32))
    o_ref[...] = x_ref[...] + _rms(mix, g_ref[...])


def _out_proj(x, mla, gla, w_a, w_b, gain, *, tm):
    T = x.shape[0]
    return pl.pallas_call(
        _out_proj_kernel,
        grid=(T // tm,),
        in_specs=[
            pl.BlockSpec((tm, D_MODEL), lambda i: (i, 0)),
            pl.BlockSpec((tm, MLA_WIDTH), lambda i: (i, 0)),
            pl.BlockSpec((tm, GLA_V_WIDTH), lambda i: (i, 0)),
            pl.BlockSpec((MLA_WIDTH, D_MODEL), lambda i: (0, 0)),
            pl.BlockSpec((GLA_V_WIDTH, D_MODEL), lambda i: (0, 0)),
            pl.BlockSpec((1, D_MODEL), lambda i: (0, 0)),
        ],
        out_specs=pl.BlockSpec((tm, D_MODEL), lambda i: (i, 0)),
        out_shape=jax.ShapeDtypeStruct((T, D_MODEL), F32),
        compiler_params=_params("parallel"),
        name="out_proj",
    )(x, mla, gla, w_a, w_b, gain)


def _mlp_kernel(x_ref, g1_ref, wu_ref, wd_ref, g2_ref, o_ref, h_ref, acc_ref):
    j = pl.program_id(1)

    @pl.when(j == 0)
    def _():
        h_ref[...] = _rms(x_ref[...], g1_ref[...]).astype(BF16)
        acc_ref[...] = jnp.zeros_like(acc_ref)

    u = jnp.dot(h_ref[...], wu_ref[...], preferred_element_type=F32)
    u = jnp.square(jnp.maximum(u, 0.0)).astype(BF16)
    acc_ref[...] += jnp.dot(u, wd_ref[...], preferred_element_type=F32)

    @pl.when(j == pl.num_programs(1) - 1)
    def _():
        o_ref[...] = x_ref[...] + _rms(acc_ref[...], g2_ref[...])


def _mlp(x, g1, w_up, w_down, g2, *, tm, tf):
    T = x.shape[0]
    return pl.pallas_call(
        _mlp_kernel,
        grid=(T // tm, D_FF // tf),
        in_specs=[
            pl.BlockSpec((tm, D_MODEL), lambda i, j: (i, 0)),
            pl.BlockSpec((1, D_MODEL), lambda i, j: (0, 0)),
            pl.BlockSpec((D_MODEL, tf), lambda i, j: (0, j)),
            pl.BlockSpec((tf, D_MODEL), lambda i, j: (j, 0)),
            pl.BlockSpec((1, D_MODEL), lambda i, j: (0, 0)),
        ],
        out_specs=pl.BlockSpec((tm, D_MODEL), lambda i, j: (i, 0)),
        out_shape=jax.ShapeDtypeStruct((T, D_MODEL), F32),
        scratch_shapes=[pltpu.VMEM((tm, D_MODEL), BF16), pltpu.VMEM((tm, D_MODEL), F32)],
        compiler_params=_params("parallel", "arbitrary"),
        name="mlp",
    )(x, g1, w_up, w_down, g2)


def _pick(n, pref):
    t = min(n, pref)
    while n % t:
        t //= 2
    return t


def _tiles(B, L):
    T = B * L
    return dict(
        in_proj=dict(tm=_pick(T, 4 * MXU_WIDTH), tn=4 * MXU_WIDTH),
        mla_prep=dict(tm=_pick(L, 2 * MXU_WIDTH)),
        attention=dict(tq=MXU_WIDTH, tk=_pick(L, 2 * MXU_WIDTH), groups=_pick(L // MXU_WIDTH, 8)),
        gla=dict(cb=_pick(L, 16 * GLA_CHUNK)),
        out_proj=dict(tm=_pick(T, 2 * MXU_WIDTH)),
        mlp=dict(tm=_pick(T, 2 * MXU_WIDTH), tf=4 * MXU_WIDTH),
    )


def _rope_tables(L):
    inv = 1.0 / (ROPE_THETA ** (jnp.arange(0, QK_ROPE, 2, dtype=F32) / QK_ROPE))
    ang = jnp.arange(L, dtype=F32)[:, None] * inv[None, :]
    return jnp.cos(ang), jnp.sin(ang)


def _prepare_weights(w_in, q_a_norm, w_q_b, kv_a_norm, w_kv_b, w_gk_f, b_gk_f, w_gk_b, b_gk_b,
                     gla_norm, w_out, pre_mix_norm, post_mix_norm, pre_mlp_norm, post_mlp_norm,
                     w_up, w_down):
    s = [0]
    for width in (Q_LORA, KV_LORA, QK_ROPE, GLA_QK_WIDTH, GLA_QK_WIDTH, GLA_V_WIDTH,
                  GLA_GATE_RANK, GLA_GATE_RANK, GLA_V_WIDTH):
        s.append(s[-1] + width)
    col = lambda i: w_in[:, s[i]:s[i + 1]]
    w_main = jnp.concatenate([col(0), col(1), col(3), col(4), col(5), col(8)], axis=1).astype(BF16)
    pad = jnp.zeros((D_MODEL, TAIL_WIDTH - QK_ROPE - 2 * GLA_GATE_RANK), w_in.dtype)
    w_tail = jnp.concatenate([col(2), col(6), col(7), pad], axis=1).astype(BF16)
    w_kv = w_kv_b.reshape(KV_LORA, MLA_HEADS, QK_NOPE + V_HEAD)
    row = lambda v: v.reshape(1, -1).astype(F32)
    return dict(
        w_main=w_main, w_tail=w_tail,
        q_norm=row(q_a_norm), kv_norm=row(kv_a_norm),
        wqT=w_q_b.T.astype(BF16),
        wk=w_kv[:, :, :QK_NOPE].reshape(KV_LORA, MLA_HEADS * QK_NOPE).astype(BF16),
        wvT=w_kv[:, :, QK_NOPE:].reshape(KV_LORA, MLA_HEADS * V_HEAD).T.astype(BF16),
        wgf=w_gk_f.astype(F32), bgf=row(b_gk_f), wgb=w_gk_b.astype(F32), bgb=row(b_gk_b),
        gla_norm=row(gla_norm),
        w_out_a=w_out[:MLA_WIDTH].astype(BF16), w_out_b=w_out[MLA_WIDTH:].astype(BF16),
        pre_mix=row(pre_mix_norm), post_mix=row(post_mix_norm),
        pre_mlp=row(pre_mlp_norm), post_mlp=row(post_mlp_norm),
        w_up=w_up.astype(BF16), w_down=w_down.astype(BF16),
    )


def _layer(x, w):
    B, L, _ = x.shape
    T = B * L
    x2 = x.reshape(T, D_MODEL)
    cos, sin = _rope_tables(L)

    t = _tiles(B, L)

    main, tail = _in_proj(x2, w["pre_mix"], w["w_main"], w["w_tail"], **t["in_proj"])
    main = main.reshape(B, L, MAIN_WIDTH)
    tail = tail.reshape(B, L, TAIL_WIDTH)

    qT, k, vT, lgf, lgb = _mla_prep(
        main, tail, w["q_norm"], w["kv_norm"], w["wqT"], w["wk"], w["wvT"],
        w["wgf"], w["bgf"], w["wgb"], w["bgb"], cos, sin, cos.T, sin.T, **t["mla_prep"])
    mla = _attention(qT, k, vT, **t["attention"])

    o_fwd = _gla(main, lgf, None, None, reverse=False, **t["gla"])
    gla = _gla(main, lgb, o_fwd, w["gla_norm"], reverse=True, **t["gla"])

    x2 = _out_proj(x2, mla.reshape(T, MLA_WIDTH), gla.reshape(T, GLA_V_WIDTH),
                   w["w_out_a"], w["w_out_b"], w["post_mix"], **t["out_proj"])
    x2 = _mlp(x2, w["pre_mlp"], w["w_up"], w["w_down"], w["post_mlp"], **t["mlp"])
    return x2.reshape(B, L, D_MODEL)


def kernel(x_prompt, x_sample, w_in, q_a_norm, w_q_b, kv_a_norm, w_kv_b, w_gk_f, b_gk_f, w_gk_b, b_gk_b, gla_norm, w_out, pre_mix_norm, post_mix_norm, pre_mlp_norm, post_mlp_norm, w_up, w_down):
    depth = w_in.shape[0]
    stacked = (w_in, q_a_norm, w_q_b, kv_a_norm, w_kv_b, w_gk_f, b_gk_f, w_gk_b, b_gk_b, gla_norm,
               w_out, pre_mix_norm, post_mix_norm, pre_mlp_norm, post_mlp_norm, w_up, w_down)
    layers = [_prepare_weights(*(t[l] for t in stacked)) for l in range(depth)]
    outs = []
    for x in (x_prompt, x_sample):
        for w in layers:
            x = _layer(x, w)
        outs.append(x)
    return tuple(outs)
```

```python
import functools
import math

import jax
import jax.numpy as jnp
from jax import lax
from jax.experimental import pallas as pl
from jax.experimental.pallas import tpu as pltpu

F32 = jnp.float32
BF16 = jnp.bfloat16

D_MODEL = 2048
MLA_HEADS = 8
QK_NOPE = 128
QK_ROPE = 64
QK_DIM = QK_NOPE + QK_ROPE
V_HEAD = 128
V_AUG = V_HEAD + 16
Q_LORA = 512
KV_LORA = 512
ROPE_THETA = 10000.0
GLA_HEADS = 4
GLA_DK = 128
GLA_DV = 256
GLA_GATE_RANK = 16
GLA_GATE_NORM = 16.0
GLA_CHUNK = 64
D_FF = 4 * D_MODEL
EPS = 1e-6

MLA_WIDTH = MLA_HEADS * V_HEAD
GLA_QK_WIDTH = GLA_HEADS * GLA_DK
GLA_V_WIDTH = GLA_HEADS * GLA_DV
MAIN_WIDTH = Q_LORA + KV_LORA + 2 * GLA_QK_WIDTH + 2 * GLA_V_WIDTH
TAIL_WIDTH = 128
HALF_ROPE = QK_ROPE // 2

Q_PRESCALE = (QK_DIM ** -0.5) * math.log2(math.e)

VMEM_LIMIT = 56 * 1024 * 1024
LANE = 128
MXU_WIDTH = 256

NT_DIMS = (((1,), (1,)), ((), ()))


def _params(*sem):
    return pltpu.CompilerParams(dimension_semantics=sem, vmem_limit_bytes=VMEM_LIMIT)


def _rms(x, gain):
    return x * lax.rsqrt(jnp.mean(x * x, axis=-1, keepdims=True) + EPS) * gain


def _in_proj_kernel(x_ref, g_ref, wm_ref, wt_ref, om_ref, ot_ref, h_ref):
    @pl.when(pl.program_id(1) == 0)
    def _():
        h = _rms(x_ref[...], g_ref[...]).astype(BF16)
        h_ref[...] = h
        ot_ref[...] = jnp.dot(h, wt_ref[...], preferred_element_type=F32)

    om_ref[...] = jnp.dot(h_ref[...], wm_ref[...], preferred_element_type=F32).astype(BF16)


def _in_proj(x, gain, w_main, w_tail, *, tm, tn):
    T = x.shape[0]
    return pl.pallas_call(
        _in_proj_kernel,
        grid=(T // tm, MAIN_WIDTH // tn),
        in_specs=[
            pl.BlockSpec((tm, D_MODEL), lambda i, j: (i, 0)),
            pl.BlockSpec((1, D_MODEL), lambda i, j: (0, 0)),
            pl.BlockSpec((D_MODEL, tn), lambda i, j: (0, j)),
            pl.BlockSpec((D_MODEL, TAIL_WIDTH), lambda i, j: (0, 0)),
        ],
        out_specs=[
            pl.BlockSpec((tm, tn), lambda i, j: (i, j)),
            pl.BlockSpec((tm, TAIL_WIDTH), lambda i, j: (i, 0)),
        ],
        out_shape=[
            jax.ShapeDtypeStruct((T, MAIN_WIDTH), BF16),
            jax.ShapeDtypeStruct((T, TAIL_WIDTH), F32),
        ],
        scratch_shapes=[pltpu.VMEM((tm, D_MODEL), BF16)],
        compiler_params=_params("parallel", "arbitrary"),
        name="in_proj",
    )(x, gain, w_main, w_tail)


def _log_sigmoid(z):
    return jnp.minimum(z, 0.0) - jnp.log(1.0 + jnp.exp(-jnp.abs(z)))


def _mla_prep_kernel(c_ref, t_ref, qn_ref, kvn_ref, wqT_ref, wk_ref, wvT_ref,
                     wgf_ref, bgf_ref, wgb_ref, bgb_ref, cos_ref, sin_ref, cosT_ref, sinT_ref,
                     qT_ref, k_ref, vT_ref, lgf_ref, lgb_ref):
    c = c_ref[0].astype(F32)
    cq = _rms(c[:, :Q_LORA], qn_ref[...]).astype(BF16)
    ckv = _rms(c[:, Q_LORA:], kvn_ref[...]).astype(BF16)

    qT = lax.dot_general(wqT_ref[...], cq, NT_DIMS, preferred_element_type=F32)
    cosT = cosT_ref[...]
    sinT = sinT_ref[...]
    for h in range(MLA_HEADS):
        r0 = h * QK_DIM
        x1 = qT[r0 + QK_NOPE:r0 + QK_NOPE + HALF_ROPE]
        x2 = qT[r0 + QK_NOPE + HALF_ROPE:r0 + QK_DIM]
        qT_ref[0, h, 0:QK_NOPE, :] = (qT[r0:r0 + QK_NOPE] * Q_PRESCALE).astype(BF16)
        qT_ref[0, h, QK_NOPE:QK_NOPE + HALF_ROPE, :] = ((x1 * cosT - x2 * sinT) * Q_PRESCALE).astype(BF16)
        qT_ref[0, h, QK_NOPE + HALF_ROPE:QK_DIM, :] = ((x2 * cosT + x1 * sinT) * Q_PRESCALE).astype(BF16)

    t = t_ref[0]
    k1 = t[:, 0:HALF_ROPE]
    k2 = t[:, HALF_ROPE:QK_ROPE]
    cos = cos_ref[...]
    sin = sin_ref[...]
    k_pe = jnp.concatenate([k1 * cos - k2 * sin, k2 * cos + k1 * sin], axis=-1).astype(BF16)
    kn = jnp.dot(ckv, wk_ref[...], preferred_element_type=F32)
    for h in range(MLA_HEADS):
        k_ref[0, h, :, 0:QK_NOPE] = kn[:, h * QK_NOPE:(h + 1) * QK_NOPE].astype(BF16)
        k_ref[0, h, :, QK_NOPE:QK_DIM] = k_pe

    vT = lax.dot_general(wvT_ref[...], ckv, NT_DIMS, preferred_element_type=F32)
    for h in range(MLA_HEADS):
        vT_ref[0, h, 0:V_HEAD, :] = vT[h * V_HEAD:(h + 1) * V_HEAD].astype(BF16)
        vT_ref[0, h, V_HEAD:V_AUG, :] = jnp.ones((V_AUG - V_HEAD, vT.shape[1]), BF16)

    gf = t[:, QK_ROPE:QK_ROPE + GLA_GATE_RANK]
    gb = t[:, QK_ROPE + GLA_GATE_RANK:QK_ROPE + 2 * GLA_GATE_RANK]
    zf = jnp.dot(gf, wgf_ref[...], preferred_element_type=F32, precision=lax.Precision.HIGHEST) + bgf_ref[...]
    zb = jnp.dot(gb, wgb_ref[...], preferred_element_type=F32, precision=lax.Precision.HIGHEST) + bgb_ref[...]
    lgf_ref[0] = _log_sigmoid(zf) * (1.0 / GLA_GATE_NORM)
    lgb_ref[0] = _log_sigmoid(zb) * (1.0 / GLA_GATE_NORM)


def _mla_prep(main, tail, q_norm, kv_norm, wqT, wk, wvT, wgf, bgf, wgb, bgb, cos, sin, cosT, sinT, *, tm):
    B, L, _ = main.shape
    const = lambda shape: pl.BlockSpec(shape, lambda b, i: (0,) * len(shape))
    return pl.pallas_call(
        _mla_prep_kernel,
        grid=(B, L // tm),
        in_specs=[
            pl.BlockSpec((1, tm, Q_LORA + KV_LORA), lambda b, i: (b, i, 0)),
            pl.BlockSpec((1, tm, TAIL_WIDTH), lambda b, i: (b, i, 0)),
            const((1, Q_LORA)),
            const((1, KV_LORA)),
            const((MLA_HEADS * QK_DIM, Q_LORA)),
            const((KV_LORA, MLA_HEADS * QK_NOPE)),
            const((MLA_HEADS * V_HEAD, KV_LORA)),
            const((GLA_GATE_RANK, GLA_QK_WIDTH)),
            const((1, GLA_QK_WIDTH)),
            const((GLA_GATE_RANK, GLA_QK_WIDTH)),
            const((1, GLA_QK_WIDTH)),
            pl.BlockSpec((tm, HALF_ROPE), lambda b, i: (i, 0)),
            pl.BlockSpec((tm, HALF_ROPE), lambda b, i: (i, 0)),
            pl.BlockSpec((HALF_ROPE, tm), lambda b, i: (0, i)),
            pl.BlockSpec((HALF_ROPE, tm), lambda b, i: (0, i)),
        ],
        out_specs=[
            pl.BlockSpec((1, MLA_HEADS, QK_DIM, tm), lambda b, i: (b, 0, 0, i)),
            pl.BlockSpec((1, MLA_HEADS, tm, QK_DIM), lambda b, i: (b, 0, i, 0)),
            pl.BlockSpec((1, MLA_HEADS, V_AUG, tm), lambda b, i: (b, 0, 0, i)),
            pl.BlockSpec((1, tm, GLA_QK_WIDTH), lambda b, i: (b, i, 0)),
            pl.BlockSpec((1, tm, GLA_QK_WIDTH), lambda b, i: (b, i, 0)),
        ],
        out_shape=[
            jax.ShapeDtypeStruct((B, MLA_HEADS, QK_DIM, L), BF16),
            jax.ShapeDtypeStruct((B, MLA_HEADS, L, QK_DIM), BF16),
            jax.ShapeDtypeStruct((B, MLA_HEADS, V_AUG, L), BF16),
            jax.ShapeDtypeStruct((B, L, GLA_QK_WIDTH), F32),
            jax.ShapeDtypeStruct((B, L, GLA_QK_WIDTH), F32),
        ],
        compiler_params=_params("parallel", "parallel"),
        name="mla_prep",
    )(main, tail, q_norm, kv_norm, wqT, wk, wvT, wgf, bgf, wgb, bgb, cos, sin, cosT, sinT)


def _attention_kernel(qT_ref, k_ref, vT_ref, o_ref, *scratch, tq, tk, nk, groups, nsteps, unroll):
    tb = groups * tq
    per_slot = lambda refs: (refs[:groups], refs[groups:])
    s_ref = per_slot(scratch[0:2 * groups])
    p_ref = per_slot(scratch[2 * groups:4 * groups])
    alpha_ref = per_slot(scratch[4 * groups:6 * groups])
    m_ref = scratch[6 * groups:7 * groups]
    bmax_ref = per_slot(scratch[7 * groups:9 * groups])
    acc_ref = [scratch[(9 + a) * groups:(10 + a) * groups] for a in range(1 + unroll)]

    def scores(t, g, slot):
        koff = pl.multiple_of((t % nk) * tk, tk)
        qoff = pl.multiple_of((t // nk) * tb + g * tq, tq)
        s = jnp.dot(k_ref[0, 0, pl.ds(koff, tk), :], qT_ref[0, 0, :, pl.ds(qoff, tq)],
                    preferred_element_type=F32)
        s_ref[slot][g][...] = s
        bmax_ref[slot][g][...] = jnp.max(s, axis=0, keepdims=True)

    def softmax(t, g, slot):
        m_old = jnp.where(t % nk == 0, -jnp.inf, m_ref[g][...])
        m_new = jnp.maximum(m_old, bmax_ref[slot][g][...])
        p_ref[slot][g][...] = jnp.exp2(s_ref[slot][g][...] - m_new).astype(BF16)
        alpha_ref[slot][g][...] = jnp.exp2(m_old - m_new)
        m_ref[g][...] = m_new

    def pv(t, g, slot, src, dst):
        koff = pl.multiple_of((t % nk) * tk, tk)
        upd = jnp.dot(vT_ref[0, 0, :, pl.ds(koff, tk)], p_ref[slot][g][...], preferred_element_type=F32)
        acc_ref[dst][g][...] = alpha_ref[slot][g][...] * acc_ref[src][g][...] + upd

    def emit(group):
        for g in range(groups):
            acc = acc_ref[1][g][...]
            qoff = pl.multiple_of(group * tb + g * tq, tq)
            o = acc[:V_HEAD] / acc[V_HEAD:V_HEAD + 1]
            o_ref[0, pl.ds(qoff, tq), :] = o.T.astype(BF16)

    for g in range(groups):
        p_ref[1][g][...] = jnp.ones((tk, tq), BF16)
        alpha_ref[1][g][...] = jnp.zeros((1, tq), F32)
        m_ref[g][...] = jnp.full((1, tq), -jnp.inf, F32)
        acc_ref[0][g][...] = jnp.zeros((V_AUG, tq), F32)
        scores(0, g, 0)

    def body(i, _):
        for h in range(unroll):
            t0 = 2 * (unroll * i + h)
            for g in range(groups):
                scores(t0 + 1, g, 1)
                pv(jnp.maximum(t0 - 1, 0), g, 1, 0, 1 + h)
                softmax(t0, g, 0)
            for g in range(groups):
                scores(jnp.minimum(t0 + 2, nsteps - 1), g, 0)
                pv(t0, g, 0, 1 + h, 0)
                softmax(t0 + 1, g, 1)

        t0 = 2 * unroll * i

        @pl.when((t0 % nk == 0) & (i > 0))
        def _():
            emit(t0 // nk - 1)

        return 0

    lax.fori_loop(0, nsteps // (2 * unroll), body, 0)
    for g in range(groups):
        pv(nsteps - 1, g, 1, 0, 1)
    emit(nsteps // nk - 1)


def _attention(qT, k, vT, *, tq, tk, groups, unroll):
    B, H, _, L = qT.shape
    nk = L // tk
    nsteps = (L // (tq * groups)) * nk
    assert nk % (2 * unroll) == 0
    lanes = lambda n: -(-n // LANE) * LANE
    in_bytes = 2 * (QK_DIM * L + L * lanes(QK_DIM) + V_AUG * L)
    out_bytes = 2 * 2 * L * V_HEAD
    scratch_bytes = groups * (2 * tk * tq * (4 + 2) + (1 + unroll) * V_AUG * tq * 4)
    fits = 2 * in_bytes + out_bytes + scratch_bytes <= (VMEM_LIMIT * 3) // 4
    mode = {} if fits else dict(pipeline_mode=pl.Buffered(1))
    return pl.pallas_call(
        functools.partial(_attention_kernel, tq=tq, tk=tk, nk=nk, groups=groups, nsteps=nsteps,
                          unroll=unroll),
        grid=(B, H),
        in_specs=[
            pl.BlockSpec((1, 1, QK_DIM, L), lambda b, h: (b, h, 0, 0), **mode),
            pl.BlockSpec((1, 1, L, QK_DIM), lambda b, h: (b, h, 0, 0), **mode),
            pl.BlockSpec((1, 1, V_AUG, L), lambda b, h: (b, h, 0, 0), **mode),
        ],
        out_specs=pl.BlockSpec((1, L, V_HEAD), lambda b, h: (b, 0, h)),
        out_shape=jax.ShapeDtypeStruct((B, L, H * V_HEAD), BF16),
        scratch_shapes=(
            [pltpu.VMEM((tk, tq), F32)] * (2 * groups)
            + [pltpu.VMEM((tk, tq), BF16)] * (2 * groups)
            + [pltpu.VMEM((1, tq), F32)] * (2 * groups)
            + [pltpu.VMEM((1, tq), F32)] * groups
            + [pltpu.VMEM((1, tq), F32)] * (2 * groups)
            + [pltpu.VMEM((V_AUG, tq), F32)] * ((1 + unroll) * groups)
        ),
        compiler_params=_params("parallel", "arbitrary"),
        name="attention",
    )(qT, k, vT)


def _gla_kernel(*refs, reverse, final, nchunks):
    if final:
        q_ref, k_ref, v_ref, lg_ref, of_ref, og_ref, gn_ref, o_ref, st_ref = refs
    else:
        q_ref, k_ref, v_ref, lg_ref, o_ref, st_ref = refs
    C = GLA_CHUNK

    @pl.when(pl.program_id(1) == 0)
    def _():
        st_ref[...] = jnp.zeros_like(st_ref)

    row = lax.broadcasted_iota(jnp.int32, (C, C), 0)
    col = lax.broadcasted_iota(jnp.int32, (C, C), 1)
    keep = (col >= row) if reverse else (col <= row)
    tri = keep.astype(BF16)
    last = 0 if reverse else C - 1
    order = [(nchunks - 1 - cc) if reverse else cc for cc in range(nchunks)]

    def gates(c):
        rows = slice(c * C, (c + 1) * C)
        g = lg_ref[0, rows, :]
        g_hi = g.astype(BF16)
        r1 = g - g_hi.astype(F32)
        g_mid = r1.astype(BF16)
        g_lo = (r1 - g_mid.astype(F32)).astype(BF16)
        b = (jnp.dot(tri, g_hi, preferred_element_type=F32)
             + jnp.dot(tri, g_mid, preferred_element_type=F32)
             + jnp.dot(tri, g_lo, preferred_element_type=F32))
        b_last = b[last:last + 1, :]
        kf = k_ref[0, rows, :].astype(F32)
        return dict(
            rows=rows,
            q_dec=(q_ref[0, rows, :].astype(F32) * (jnp.exp(b) * (GLA_DK ** -0.5))).astype(BF16),
            k_inv=(kf * jnp.exp(-b)).astype(BF16),
            k_tail=(kf * jnp.exp(b_last - b)).astype(BF16),
            decay=jnp.exp(b_last),
        )

    def products(ch):
        a, u = [], []
        for h in range(GLA_HEADS):
            ks = slice(h * GLA_DK, (h + 1) * GLA_DK)
            v = v_ref[0, ch["rows"], h * GLA_DV:(h + 1) * GLA_DV]
            a.append(lax.dot_general(ch["q_dec"][:, ks], ch["k_inv"][:, ks], NT_DIMS,
                                     preferred_element_type=F32))
            vT = v.astype(F32).T.astype(BF16)
            u.append(jnp.dot(vT, ch["k_tail"][:, ks], preferred_element_type=F32))
        ch["a"], ch["u"] = a, u

    def outputs(ch):
        rows = ch["rows"]
        for h in range(GLA_HEADS):
            ks = slice(h * GLA_DK, (h + 1) * GLA_DK)
            vs = slice(h * GLA_DV, (h + 1) * GLA_DV)
            a = jnp.where(keep, ch["a"][h], 0.0).astype(BF16)
            st = st_ref[h]
            o = (jnp.dot(a, v_ref[0, rows, vs], preferred_element_type=F32)
                 + lax.dot_general(ch["q_dec"][:, ks], st.astype(BF16), NT_DIMS,
                                   preferred_element_type=F32))
            st_ref[h] = st * ch["decay"][:, ks] + ch["u"][h]
            if final:
                o = o + of_ref[0, rows, vs]
                o = _rms(o, gn_ref[...])
                gate = og_ref[0, rows, vs].astype(F32)
                o = o * (gate * jax.nn.sigmoid(gate))
                o_ref[0, rows, vs] = o.astype(BF16)
            else:
                o_ref[0, rows, vs] = o

    chunks = {}
    for i in range(-2, nchunks):
        if i + 2 < nchunks:
            chunks[i + 2] = gates(order[i + 2])
        if 0 <= i + 1 < nchunks:
            products(chunks[i + 1])
        if i >= 0:
            outputs(chunks.pop(i))


def _gla(main, lg, o_fwd, gla_norm, *, reverse, cb):
    B, L, _ = main.shape
    nb = L // cb
    final = o_fwd is not None
    blk = (lambda i: nb - 1 - i) if reverse else (lambda i: i)
    in_specs = [
        pl.BlockSpec((1, cb, GLA_QK_WIDTH), lambda b, i: (b, blk(i), 2)),
        pl.BlockSpec((1, cb, GLA_QK_WIDTH), lambda b, i: (b, blk(i), 3)),
        pl.BlockSpec((1, cb, GLA_V_WIDTH), lambda b, i: (b, blk(i), 2)),
        pl.BlockSpec((1, cb, GLA_QK_WIDTH), lambda b, i: (b, blk(i), 0)),
    ]
    args = [main, main, main, lg]
    if final:
        in_specs += [
            pl.BlockSpec((1, cb, GLA_V_WIDTH), lambda b, i: (b, blk(i), 0)),
            pl.BlockSpec((1, cb, GLA_V_WIDTH), lambda b, i: (b, blk(i), 3)),
            pl.BlockSpec((1, GLA_DV), lambda b, i: (0, 0)),
        ]
        args += [o_fwd, main, gla_norm]
    return pl.pallas_call(
        functools.partial(_gla_kernel, reverse=reverse, final=final, nchunks=cb // GLA_CHUNK),
        grid=(B, nb),
        in_specs=in_specs,
        out_specs=pl.BlockSpec((1, cb, GLA_V_WIDTH), lambda b, i: (b, blk(i), 0)),
        out_shape=jax.ShapeDtypeStruct((B, L, GLA_V_WIDTH), BF16 if final else F32),
        scratch_shapes=[pltpu.VMEM((GLA_HEADS, GLA_DV, GLA_DK), F32)],
        compiler_params=_params("parallel", "arbitrary"),
        name="gla_bwd" if reverse else "gla_fwd",
    )(*args)


def _out_proj_kernel(x_ref, mla_ref, gla_ref, wa_ref, wb_ref, g_ref, o_ref):
    mix = (jnp.dot(mla_ref[...], wa_ref[...], preferred_element_type=F32)
           + jnp.dot(gla_ref[...], wb_ref[...], preferred_element_type=F32))
    o_ref[...] = x_ref[...] + _rms(mix, g_ref[...])


def _out_proj(x, mla, gla, w_a, w_b, gain, *, tm):
    T = x.shape[0]
    return pl.pallas_call(
        _out_proj_kernel,
        grid=(T // tm,),
        in_specs=[
            pl.BlockSpec((tm, D_MODEL), lambda i: (i, 0)),
            pl.BlockSpec((tm, MLA_WIDTH), lambda i: (i, 0)),
            pl.BlockSpec((tm, GLA_V_WIDTH), lambda i: (i, 0)),
            pl.BlockSpec((MLA_WIDTH, D_MODEL), lambda i: (0, 0)),
            pl.BlockSpec((GLA_V_WIDTH, D_MODEL), lambda i: (0, 0)),
            pl.BlockSpec((1, D_MODEL), lambda i: (0, 0)),
        ],
        out_specs=pl.BlockSpec((tm, D_MODEL), lambda i: (i, 0)),
        out_shape=jax.ShapeDtypeStruct((T, D_MODEL), F32),
        compiler_params=_params("parallel"),
        name="out_proj",
    )(x, mla, gla, w_a, w_b, gain)


def _mlp_kernel(x_ref, g1_ref, wu_ref, wd_ref, g2_ref, o_ref, h_ref, acc_ref):
    j = pl.program_id(1)

    @pl.when(j == 0)
    def _():
        h_ref[...] = _rms(x_ref[...], g1_ref[...]).astype(BF16)
        acc_ref[...] = jnp.zeros_like(acc_ref)

    u = jnp.dot(h_ref[...], wu_ref[...], preferred_element_type=F32)
    u = jnp.square(jnp.maximum(u, 0.0)).astype(BF16)
    acc_ref[...] += jnp.dot(u, wd_ref[...], preferred_element_type=F32)

    @pl.when(j == pl.num_programs(1) - 1)
    def _():
        o_ref[...] = x_ref[...] + _rms(acc_ref[...], g2_ref[...])


def _mlp(x, g1, w_up, w_down, g2, *, tm, tf):
    T = x.shape[0]
    return pl.pallas_call(
        _mlp_kernel,
        grid=(T // tm, D_FF // tf),
        in_specs=[
            pl.BlockSpec((tm, D_MODEL), lambda i, j: (i, 0)),
            pl.BlockSpec((1, D_MODEL), lambda i, j: (0, 0)),
            pl.BlockSpec((D_MODEL, tf), lambda i, j: (0, j)),
            pl.BlockSpec((tf, D_MODEL), lambda i, j: (j, 0)),
            pl.BlockSpec((1, D_MODEL), lambda i, j: (0, 0)),
        ],
        out_specs=pl.BlockSpec((tm, D_MODEL), lambda i, j: (i, 0)),
        out_shape=jax.ShapeDtypeStruct((T, D_MODEL), F32),
        scratch_shapes=[pltpu.VMEM((tm, D_MODEL), BF16), pltpu.VMEM((tm, D_MODEL), F32)],
        compiler_params=_params("parallel", "arbitrary"),
        name="mlp",
    )(x, g1, w_up, w_down, g2)


def _pick(n, pref):
    t = min(n, pref)
    while n % t:
        t //= 2
    return t


def _tiles(B, L):
    T = B * L
    tk = _pick(L, 2 * MXU_WIDTH)
    return dict(
        in_proj=dict(tm=_pick(T, 4 * MXU_WIDTH), tn=4 * MXU_WIDTH),
        mla_prep=dict(tm=_pick(L, 2 * MXU_WIDTH)),
        attention=dict(tq=MXU_WIDTH, tk=tk, groups=_pick(L // MXU_WIDTH, 8),
                       unroll=2 if (L // tk) % 4 == 0 else 1),
        gla=dict(cb=_pick(L, 16 * GLA_CHUNK)),
        out_proj=dict(tm=_pick(T, 2 * MXU_WIDTH)),
        mlp=dict(tm=_pick(T, 2 * MXU_WIDTH), tf=4 * MXU_WIDTH),
    )


def _rope_tables(L):
    inv = 1.0 / (ROPE_THETA ** (jnp.arange(0, QK_ROPE, 2, dtype=F32) / QK_ROPE))
    ang = jnp.arange(L, dtype=F32)[:, None] * inv[None, :]
    return jnp.cos(ang), jnp.sin(ang)


def _prepare_weights(w_in, q_a_norm, w_q_b, kv_a_norm, w_kv_b, w_gk_f, b_gk_f, w_gk_b, b_gk_b,
                     gla_norm, w_out, pre_mix_norm, post_mix_norm, pre_mlp_norm, post_mlp_norm,
                     w_up, w_down):
    s = [0]
    for width in (Q_LORA, KV_LORA, QK_ROPE, GLA_QK_WIDTH, GLA_QK_WIDTH, GLA_V_WIDTH,
                  GLA_GATE_RANK, GLA_GATE_RANK, GLA_V_WIDTH):
        s.append(s[-1] + width)
    col = lambda i: w_in[:, s[i]:s[i + 1]]
    w_main = jnp.concatenate([col(0), col(1), col(3), col(4), col(5), col(8)], axis=1).astype(BF16)
    pad = jnp.zeros((D_MODEL, TAIL_WIDTH - QK_ROPE - 2 * GLA_GATE_RANK), w_in.dtype)
    w_tail = jnp.concatenate([col(2), col(6), col(7), pad], axis=1).astype(BF16)
    w_kv = w_kv_b.reshape(KV_LORA, MLA_HEADS, QK_NOPE + V_HEAD)
    row = lambda v: v.reshape(1, -1).astype(F32)
    return dict(
        w_main=w_main, w_tail=w_tail,
        q_norm=row(q_a_norm), kv_norm=row(kv_a_norm),
        wqT=w_q_b.T.astype(BF16),
        wk=w_kv[:, :, :QK_NOPE].reshape(KV_LORA, MLA_HEADS * QK_NOPE).astype(BF16),
        wvT=w_kv[:, :, QK_NOPE:].reshape(KV_LORA, MLA_HEADS * V_HEAD).T.astype(BF16),
        wgf=w_gk_f.astype(F32), bgf=row(b_gk_f), wgb=w_gk_b.astype(F32), bgb=row(b_gk_b),
        gla_norm=row(gla_norm),
        w_out_a=w_out[:MLA_WIDTH].astype(BF16), w_out_b=w_out[MLA_WIDTH:].astype(BF16),
        pre_mix=row(pre_mix_norm), post_mix=row(post_mix_norm),
        pre_mlp=row(pre_mlp_norm), post_mlp=row(post_mlp_norm),
        w_up=w_up.astype(BF16), w_down=w_down.astype(BF16),
    )


def _layer(x, w):
    B, L, _ = x.shape
    T = B * L
    x2 = x.reshape(T, D_MODEL)
    cos, sin = _rope_tables(L)

    t = _tiles(B, L)

    main, tail = _in_proj(x2, w["pre_mix"], w["w_main"], w["w_tail"], **t["in_proj"])
    main = main.reshape(B, L, MAIN_WIDTH)
    tail = tail.reshape(B, L, TAIL_WIDTH)

    qT, k, vT, lgf, lgb = _mla_prep(
        main, tail, w["q_norm"], w["kv_norm"], w["wqT"], w["wk"], w["wvT"],
        w["wgf"], w["bgf"], w["wgb"], w["bgb"], cos, sin, cos.T, sin.T, **t["mla_prep"])
    mla = _attention(qT, k, vT, **t["attention"])

    o_fwd = _gla(main, lgf, None, None, reverse=False, **t["gla"])
    gla = _gla(main, lgb, o_fwd, w["gla_norm"], reverse=True, **t["gla"])

    x2 = _out_proj(x2, mla.reshape(T, MLA_WIDTH), gla.reshape(T, GLA_V_WIDTH),
                   w["w_out_a"], w["w_out_b"], w["post_mix"], **t["out_proj"])
    x2 = _mlp(x2, w["pre_mlp"], w["w_up"], w["w_down"], w["post_mlp"], **t["mlp"])
    return x2.reshape(B, L, D_MODEL)


def kernel(x_prompt, x_sample, w_in, q_a_norm, w_q_b, kv_a_norm, w_kv_b, w_gk_f, b_gk_f, w_gk_b, b_gk_b, gla_norm, w_out, pre_mix_norm, post_mix_norm, pre_mlp_norm, post_mlp_norm, w_up, w_down):
    depth = w_in.shape[0]
    stacked = (w_in, q_a_norm, w_q_b, kv_a_norm, w_kv_b, w_gk_f, b_gk_f, w_gk_b, b_gk_b, gla_norm,
               w_out, pre_mix_norm, post_mix_norm, pre_mlp_norm, post_mlp_norm, w_up, w_down)
    layers = [_prepare_weights(*(t[l] for t in stacked)) for l in range(depth)]
    outs = []
    for x in (x_prompt, x_sample):
        for w in layers:
            x = _layer(x, w)
        outs.append(x)
    return tuple(outs)
```

```python
import functools
import math

import jax
import jax.numpy as jnp
from jax import lax
from jax.experimental import pallas as pl
from jax.experimental.pallas import tpu as pltpu

F32 = jnp.float32
BF16 = jnp.bfloat16

D_MODEL = 2048
MLA_HEADS = 8
QK_NOPE = 128
QK_ROPE = 64
QK_DIM = QK_NOPE + QK_ROPE
V_HEAD = 128
V_AUG = V_HEAD + 16
Q_LORA = 512
KV_LORA = 512
ROPE_THETA = 10000.0
GLA_HEADS = 4
GLA_DK = 128
GLA_DV = 256
GLA_GATE_RANK = 16
GLA_GATE_NORM = 16.0
GLA_CHUNK = 64
D_FF = 4 * D_MODEL
EPS = 1e-6

MLA_WIDTH = MLA_HEADS * V_HEAD
GLA_QK_WIDTH = GLA_HEADS * GLA_DK
GLA_V_WIDTH = GLA_HEADS * GLA_DV
MAIN_WIDTH = Q_LORA + KV_LORA + 2 * GLA_QK_WIDTH + 2 * GLA_V_WIDTH
TAIL_WIDTH = 128
HALF_ROPE = QK_ROPE // 2

Q_PRESCALE = (QK_DIM ** -0.5) * math.log2(math.e)

VMEM_LIMIT = 56 * 1024 * 1024
LANE = 128
MXU_WIDTH = 256

NT_DIMS = (((1,), (1,)), ((), ()))


def _params(*sem):
    return pltpu.CompilerParams(dimension_semantics=sem, vmem_limit_bytes=VMEM_LIMIT)


def _rms(x, gain):
    return x * lax.rsqrt(jnp.mean(x * x, axis=-1, keepdims=True) + EPS) * gain


def _in_proj_kernel(x_ref, g_ref, wm_ref, wt_ref, om_ref, ot_ref, h_ref):
    j = pl.program_id(1)

    @pl.when(j == 0)
    def _():
        h = _rms(x_ref[...], g_ref[...]).astype(BF16)
        h_ref[...] = h
        om_ref[...] = jnp.dot(h, wm_ref[...], preferred_element_type=F32).astype(BF16)
        ot_ref[...] = jnp.dot(h, wt_ref[...], preferred_element_type=F32)

    @pl.when(j != 0)
    def _():
        om_ref[...] = jnp.dot(h_ref[...], wm_ref[...], preferred_element_type=F32).astype(BF16)


def _in_proj(x, gain, w_main, w_tail, *, tm, tn):
    T = x.shape[0]
    return pl.pallas_call(
        _in_proj_kernel,
        grid=(T // tm, MAIN_WIDTH // tn),
        in_specs=[
            pl.BlockSpec((tm, D_MODEL), lambda i, j: (i, 0)),
            pl.BlockSpec((1, D_MODEL), lambda i, j: (0, 0)),
            pl.BlockSpec((D_MODEL, tn), lambda i, j: (0, j)),
            pl.BlockSpec((D_MODEL, TAIL_WIDTH), lambda i, j: (0, 0)),
        ],
        out_specs=[
            pl.BlockSpec((tm, tn), lambda i, j: (i, j)),
            pl.BlockSpec((tm, TAIL_WIDTH), lambda i, j: (i, 0)),
        ],
        out_shape=[
            jax.ShapeDtypeStruct((T, MAIN_WIDTH), BF16),
            jax.ShapeDtypeStruct((T, TAIL_WIDTH), F32),
        ],
        scratch_shapes=[pltpu.VMEM((tm, D_MODEL), BF16)],
        compiler_params=_params("parallel", "arbitrary"),
        name="in_proj",
    )(x, gain, w_main, w_tail)


def _log_sigmoid(z):
    return jnp.minimum(z, 0.0) - jnp.log(1.0 + jnp.exp(-jnp.abs(z)))


def _dot_bf16x3(a, b):
    a_hi = a.astype(BF16)
    a_lo = (a - a_hi.astype(F32)).astype(BF16)
    b_hi = b.astype(BF16)
    b_lo = (b - b_hi.astype(F32)).astype(BF16)
    return (jnp.dot(a_hi, b_hi, preferred_element_type=F32)
            + jnp.dot(a_hi, b_lo, preferred_element_type=F32)
            + jnp.dot(a_lo, b_hi, preferred_element_type=F32))


def _mla_prep_kernel(c_ref, t_ref, qn_ref, kvn_ref, wqT_ref, wk_ref, wvT_ref,
                     wgf_ref, bgf_ref, wgb_ref, bgb_ref, cos_ref, sin_ref, cosT_ref, sinT_ref,
                     qT_ref, k_ref, vT_ref, lgf_ref, lgb_ref):
    c = c_ref[0].astype(F32)
    cq = _rms(c[:, :Q_LORA], qn_ref[...]).astype(BF16)
    ckv = _rms(c[:, Q_LORA:], kvn_ref[...]).astype(BF16)

    qT = lax.dot_general(wqT_ref[...], cq, NT_DIMS, preferred_element_type=F32)
    cosT = cosT_ref[...]
    sinT = sinT_ref[...]
    for h in range(MLA_HEADS):
        r0 = h * QK_DIM
        x1 = qT[r0 + QK_NOPE:r0 + QK_NOPE + HALF_ROPE]
        x2 = qT[r0 + QK_NOPE + HALF_ROPE:r0 + QK_DIM]
        qT_ref[0, h, 0:QK_NOPE, :] = (qT[r0:r0 + QK_NOPE] * Q_PRESCALE).astype(BF16)
        qT_ref[0, h, QK_NOPE:QK_NOPE + HALF_ROPE, :] = ((x1 * cosT - x2 * sinT) * Q_PRESCALE).astype(BF16)
        qT_ref[0, h, QK_NOPE + HALF_ROPE:QK_DIM, :] = ((x2 * cosT + x1 * sinT) * Q_PRESCALE).astype(BF16)

    t = t_ref[0]
    k1 = t[:, 0:HALF_ROPE]
    k2 = t[:, HALF_ROPE:QK_ROPE]
    cos = cos_ref[...]
    sin = sin_ref[...]
    k_pe = jnp.concatenate([k1 * cos - k2 * sin, k2 * cos + k1 * sin], axis=-1).astype(BF16)
    kn = jnp.dot(ckv, wk_ref[...], preferred_element_type=F32)
    for h in range(MLA_HEADS):
        k_ref[0, h, :, 0:QK_NOPE] = kn[:, h * QK_NOPE:(h + 1) * QK_NOPE].astype(BF16)
        k_ref[0, h, :, QK_NOPE:QK_DIM] = k_pe

    vT = lax.dot_general(wvT_ref[...], ckv, NT_DIMS, preferred_element_type=F32)
    for h in range(MLA_HEADS):
        vT_ref[0, h, 0:V_HEAD, :] = vT[h * V_HEAD:(h + 1) * V_HEAD].astype(BF16)
        vT_ref[0, h, V_HEAD:V_AUG, :] = jnp.ones((V_AUG - V_HEAD, vT.shape[1]), BF16)

    gf = t[:, QK_ROPE:QK_ROPE + GLA_GATE_RANK]
    gb = t[:, QK_ROPE + GLA_GATE_RANK:QK_ROPE + 2 * GLA_GATE_RANK]
    zf = _dot_bf16x3(gf, wgf_ref[...]) + bgf_ref[...]
    zb = _dot_bf16x3(gb, wgb_ref[...]) + bgb_ref[...]
    lgf_ref[0] = _log_sigmoid(zf) * (1.0 / GLA_GATE_NORM)
    lgb_ref[0] = _log_sigmoid(zb) * (1.0 / GLA_GATE_NORM)


def _mla_prep(main, tail, q_norm, kv_norm, wqT, wk, wvT, wgf, bgf, wgb, bgb, cos, sin, cosT, sinT, *, tm):
    B, L, _ = main.shape
    const = lambda shape: pl.BlockSpec(shape, lambda b, i: (0,) * len(shape))
    return pl.pallas_call(
        _mla_prep_kernel,
        grid=(B, L // tm),
        in_specs=[
            pl.BlockSpec((1, tm, Q_LORA + KV_LORA), lambda b, i: (b, i, 0)),
            pl.BlockSpec((1, tm, TAIL_WIDTH), lambda b, i: (b, i, 0)),
            const((1, Q_LORA)),
            const((1, KV_LORA)),
            const((MLA_HEADS * QK_DIM, Q_LORA)),
            const((KV_LORA, MLA_HEADS * QK_NOPE)),
            const((MLA_HEADS * V_HEAD, KV_LORA)),
            const((GLA_GATE_RANK, GLA_QK_WIDTH)),
            const((1, GLA_QK_WIDTH)),
            const((GLA_GATE_RANK, GLA_QK_WIDTH)),
            const((1, GLA_QK_WIDTH)),
            pl.BlockSpec((tm, HALF_ROPE), lambda b, i: (i, 0)),
            pl.BlockSpec((tm, HALF_ROPE), lambda b, i: (i, 0)),
            pl.BlockSpec((HALF_ROPE, tm), lambda b, i: (0, i)),
            pl.BlockSpec((HALF_ROPE, tm), lambda b, i: (0, i)),
        ],
        out_specs=[
            pl.BlockSpec((1, MLA_HEADS, QK_DIM, tm), lambda b, i: (b, 0, 0, i)),
            pl.BlockSpec((1, MLA_HEADS, tm, QK_DIM), lambda b, i: (b, 0, i, 0)),
            pl.BlockSpec((1, MLA_HEADS, V_AUG, tm), lambda b, i: (b, 0, 0, i)),
            pl.BlockSpec((1, tm, GLA_QK_WIDTH), lambda b, i: (b, i, 0)),
            pl.BlockSpec((1, tm, GLA_QK_WIDTH), lambda b, i: (b, i, 0)),
        ],
        out_shape=[
            jax.ShapeDtypeStruct((B, MLA_HEADS, QK_DIM, L), BF16),
            jax.ShapeDtypeStruct((B, MLA_HEADS, L, QK_DIM), BF16),
            jax.ShapeDtypeStruct((B, MLA_HEADS, V_AUG, L), BF16),
            jax.ShapeDtypeStruct((B, L, GLA_QK_WIDTH), F32),
            jax.ShapeDtypeStruct((B, L, GLA_QK_WIDTH), F32),
        ],
        compiler_params=_params("parallel", "parallel"),
        name="mla_prep",
    )(main, tail, q_norm, kv_norm, wqT, wk, wvT, wgf, bgf, wgb, bgb, cos, sin, cosT, sinT)


def _attention_kernel(qT_ref, k_ref, vT_ref, o_ref, *scratch, tq, tk, nk, groups, nsteps, unroll):
    tb = groups * tq
    per_slot = lambda refs: (refs[:groups], refs[groups:])
    s_ref = per_slot(scratch[0:2 * groups])
    p_ref = per_slot(scratch[2 * groups:4 * groups])
    alpha_ref = per_slot(scratch[4 * groups:6 * groups])
    m_ref = scratch[6 * groups:7 * groups]
    bmax_ref = per_slot(scratch[7 * groups:9 * groups])
    acc_ref = [scratch[(9 + a) * groups:(10 + a) * groups] for a in range(1 + unroll)]

    def scores(t, g, slot):
        koff = pl.multiple_of((t % nk) * tk, tk)
        qoff = pl.multiple_of((t // nk) * tb + g * tq, tq)
        s = jnp.dot(k_ref[0, 0, pl.ds(koff, tk), :], qT_ref[0, 0, :, pl.ds(qoff, tq)],
                    preferred_element_type=F32)
        s_ref[slot][g][...] = s
        bmax_ref[slot][g][...] = jnp.max(s, axis=0, keepdims=True)

    def softmax(t, g, slot):
        m_old = jnp.where(t % nk == 0, -jnp.inf, m_ref[g][...])
        m_new = jnp.maximum(m_old, bmax_ref[slot][g][...])
        p_ref[slot][g][...] = jnp.exp2(s_ref[slot][g][...] - m_new).astype(BF16)
        alpha_ref[slot][g][...] = jnp.exp2(m_old - m_new)
        m_ref[g][...] = m_new

    def pv(t, g, slot, src, dst):
        koff = pl.multiple_of((t % nk) * tk, tk)
        upd = jnp.dot(vT_ref[0, 0, :, pl.ds(koff, tk)], p_ref[slot][g][...], preferred_element_type=F32)
        acc_ref[dst][g][...] = alpha_ref[slot][g][...] * acc_ref[src][g][...] + upd

    def emit(group):
        for g in range(groups):
            acc = acc_ref[1][g][...]
            qoff = pl.multiple_of(group * tb + g * tq, tq)
            o = acc[:V_HEAD] / acc[V_HEAD:V_HEAD + 1]
            o_ref[0, pl.ds(qoff, tq), :] = o.T.astype(BF16)

    for g in range(groups):
        p_ref[1][g][...] = jnp.ones((tk, tq), BF16)
        alpha_ref[1][g][...] = jnp.zeros((1, tq), F32)
        m_ref[g][...] = jnp.full((1, tq), -jnp.inf, F32)
        acc_ref[0][g][...] = jnp.zeros((V_AUG, tq), F32)
        scores(0, g, 0)

    def body(i, _):
        for h in range(unroll):
            t0 = 2 * (unroll * i + h)
            for g in range(groups):
                scores(t0 + 1, g, 1)
                pv(jnp.maximum(t0 - 1, 0), g, 1, 0, 1 + h)
                softmax(t0, g, 0)
            for g in range(groups):
                scores(jnp.minimum(t0 + 2, nsteps - 1), g, 0)
                pv(t0, g, 0, 1 + h, 0)
                softmax(t0 + 1, g, 1)

        t0 = 2 * unroll * i

        @pl.when((t0 % nk == 0) & (i > 0))
        def _():
            emit(t0 // nk - 1)

        return 0

    lax.fori_loop(0, nsteps // (2 * unroll), body, 0)
    for g in range(groups):
        pv(nsteps - 1, g, 1, 0, 1)
    emit(nsteps // nk - 1)


def _attention(qT, k, vT, *, tq, tk, groups, unroll):
    B, H, _, L = qT.shape
    nk = L // tk
    nsteps = (L // (tq * groups)) * nk
    assert nk % (2 * unroll) == 0
    lanes = lambda n: -(-n // LANE) * LANE
    in_bytes = 2 * (QK_DIM * L + L * lanes(QK_DIM) + V_AUG * L)
    out_bytes = 2 * 2 * L * V_HEAD
    scratch_bytes = groups * (2 * tk * tq * (4 + 2) + (1 + unroll) * V_AUG * tq * 4)
    fits = 2 * in_bytes + out_bytes + scratch_bytes <= (VMEM_LIMIT * 3) // 4
    mode = {} if fits else dict(pipeline_mode=pl.Buffered(1))
    return pl.pallas_call(
        functools.partial(_attention_kernel, tq=tq, tk=tk, nk=nk, groups=groups, nsteps=nsteps,
                          unroll=unroll),
        grid=(B, H),
        in_specs=[
            pl.BlockSpec((1, 1, QK_DIM, L), lambda b, h: (b, h, 0, 0), **mode),
            pl.BlockSpec((1, 1, L, QK_DIM), lambda b, h: (b, h, 0, 0), **mode),
            pl.BlockSpec((1, 1, V_AUG, L), lambda b, h: (b, h, 0, 0), **mode),
        ],
        out_specs=pl.BlockSpec((1, L, V_HEAD), lambda b, h: (b, 0, h)),
        out_shape=jax.ShapeDtypeStruct((B, L, H * V_HEAD), BF16),
        scratch_shapes=(
            [pltpu.VMEM((tk, tq), F32)] * (2 * groups)
            + [pltpu.VMEM((tk, tq), BF16)] * (2 * groups)
            + [pltpu.VMEM((1, tq), F32)] * (2 * groups)
            + [pltpu.VMEM((1, tq), F32)] * groups
            + [pltpu.VMEM((1, tq), F32)] * (2 * groups)
            + [pltpu.VMEM((V_AUG, tq), F32)] * ((1 + unroll) * groups)
        ),
        compiler_params=_params("parallel", "arbitrary"),
        name="attention",
    )(qT, k, vT)


def _gla_kernel(*refs, reverse, final, nchunks):
    if final:
        q_ref, k_ref, v_ref, lg_ref, of_ref, og_ref, gn_ref, o_ref, st_ref = refs
    else:
        q_ref, k_ref, v_ref, lg_ref, o_ref, st_ref = refs
    C = GLA_CHUNK

    @pl.when(pl.program_id(1) == 0)
    def _():
        st_ref[...] = jnp.zeros_like(st_ref)

    row = lax.broadcasted_iota(jnp.int32, (C, C), 0)
    col = lax.broadcasted_iota(jnp.int32, (C, C), 1)
    keep = (col >= row) if reverse else (col <= row)
    tri = keep.astype(BF16)
    last = 0 if reverse else C - 1
    order = [(nchunks - 1 - cc) if reverse else cc for cc in range(nchunks)]

    def gates(c):
        rows = slice(c * C, (c + 1) * C)
        g = lg_ref[0, rows, :]
        g_hi = g.astype(BF16)
        r1 = g - g_hi.astype(F32)
        g_mid = r1.astype(BF16)
        g_lo = (r1 - g_mid.astype(F32)).astype(BF16)
        b = (jnp.dot(tri, g_hi, preferred_element_type=F32)
             + jnp.dot(tri, g_mid, preferred_element_type=F32)
             + jnp.dot(tri, g_lo, preferred_element_type=F32))
        b_last = b[last:last + 1, :]
        kf = k_ref[0, rows, :].astype(F32)
        return dict(
            rows=rows,
            q_dec=(q_ref[0, rows, :].astype(F32) * (jnp.exp(b) * (GLA_DK ** -0.5))).astype(BF16),
            k_inv=(kf * jnp.exp(-b)).astype(BF16),
            k_tail=(kf * jnp.exp(b_last - b)).astype(BF16),
            decay=jnp.exp(b_last),
        )

    def products(ch):
        a, u = [], []
        for h in range(GLA_HEADS):
            ks = slice(h * GLA_DK, (h + 1) * GLA_DK)
            v = v_ref[0, ch["rows"], h * GLA_DV:(h + 1) * GLA_DV]
            a.append(lax.dot_general(ch["q_dec"][:, ks], ch["k_inv"][:, ks], NT_DIMS,
                                     preferred_element_type=F32))
            vT = v.astype(F32).T.astype(BF16)
            u.append(jnp.dot(vT, ch["k_tail"][:, ks], preferred_element_type=F32))
        ch["a"], ch["u"] = a, u

    def outputs(ch):
        rows = ch["rows"]
        for h in range(GLA_HEADS):
            ks = slice(h * GLA_DK, (h + 1) * GLA_DK)
            vs = slice(h * GLA_DV, (h + 1) * GLA_DV)
            a = jnp.where(keep, ch["a"][h], 0.0).astype(BF16)
            st = st_ref[h]
            o = (jnp.dot(a, v_ref[0, rows, vs], preferred_element_type=F32)
                 + lax.dot_general(ch["q_dec"][:, ks], st.astype(BF16), NT_DIMS,
                                   preferred_element_type=F32))
            st_ref[h] = st * ch["decay"][:, ks] + ch["u"][h]
            if final:
                o = o + of_ref[0, rows, vs]
                o = _rms(o, gn_ref[...])
                gate = og_ref[0, rows, vs].astype(F32)
                o = o * (gate * jax.nn.sigmoid(gate))
                o_ref[0, rows, vs] = o.astype(BF16)
            else:
                o_ref[0, rows, vs] = o

    chunks = {}
    for i in range(-2, nchunks):
        if i + 2 < nchunks:
            chunks[i + 2] = gates(order[i + 2])
        if 0 <= i + 1 < nchunks:
            products(chunks[i + 1])
        if i >= 0:
            outputs(chunks.pop(i))


def _gla(main, lg, o_fwd, gla_norm, *, reverse, cb):
    B, L, _ = main.shape
    nb = L // cb
    final = o_fwd is not None
    blk = (lambda i: nb - 1 - i) if reverse else (lambda i: i)
    in_specs = [
        pl.BlockSpec((1, cb, GLA_QK_WIDTH), lambda b, i: (b, blk(i), 2)),
        pl.BlockSpec((1, cb, GLA_QK_WIDTH), lambda b, i: (b, blk(i), 3)),
        pl.BlockSpec((1, cb, GLA_V_WIDTH), lambda b, i: (b, blk(i), 2)),
        pl.BlockSpec((1, cb, GLA_QK_WIDTH), lambda b, i: (b, blk(i), 0)),
    ]
    args = [main, main, main, lg]
    if final:
        in_specs += [
            pl.BlockSpec((1, cb, GLA_V_WIDTH), lambda b, i: (b, blk(i), 0)),
            pl.BlockSpec((1, cb, GLA_V_WIDTH), lambda b, i: (b, blk(i), 3)),
            pl.BlockSpec((1, GLA_DV), lambda b, i: (0, 0)),
        ]
        args += [o_fwd, main, gla_norm]
    return pl.pallas_call(
        functools.partial(_gla_kernel, reverse=reverse, final=final, nchunks=cb // GLA_CHUNK),
        grid=(B, nb),
        in_specs=in_specs,
        out_specs=pl.BlockSpec((1, cb, GLA_V_WIDTH), lambda b, i: (b, blk(i), 0)),
        out_shape=jax.ShapeDtypeStruct((B, L, GLA_V_WIDTH), BF16 if final else F32),
        scratch_shapes=[pltpu.VMEM((GLA_HEADS, GLA_DV, GLA_DK), F32)],
        compiler_params=_params("parallel", "arbitrary"),
        name="gla_bwd" if reverse else "gla_fwd",
    )(*args)


def _out_proj_kernel(x_ref, mla_ref, gla_ref, wa_ref, wb_ref, g_ref, o_ref):
    mix = (jnp.dot(mla_ref[...], wa_ref[...], preferred_element_type=F32)
           + jnp.dot(gla_ref[...], wb_ref[...], preferred_element_type=F32))
    o_ref[...] = x_ref[...] + _rms(mix, g_ref[...])


def _out_proj(x, mla, gla, w_a, w_b, gain, *, tm):
    T = x.shape[0]
    return pl.pallas_call(
        _out_proj_kernel,
        grid=(T // tm,),
        in_specs=[
            pl.BlockSpec((tm, D_MODEL), lambda i: (i, 0)),
            pl.BlockSpec((tm, MLA_WIDTH), lambda i: (i, 0)),
            pl.BlockSpec((tm, GLA_V_WIDTH), lambda i: (i, 0)),
            pl.BlockSpec((MLA_WIDTH, D_MODEL), lambda i: (0, 0)),
            pl.BlockSpec((GLA_V_WIDTH, D_MODEL), lambda i: (0, 0)),
            pl.BlockSpec((1, D_MODEL), lambda i: (0, 0)),
        ],
        out_specs=pl.BlockSpec((tm, D_MODEL), lambda i: (i, 0)),
        out_shape=jax.ShapeDtypeStruct((T, D_MODEL), F32),
        compiler_params=_params("parallel"),
        name="out_proj",
    )(x, mla, gla, w_a, w_b, gain)


def _mlp_kernel(x_ref, g1_ref, wu_ref, wd_ref, g2_ref, o_ref, h_ref, acc_ref):
    j = pl.program_id(1)
    last = pl.num_programs(1) - 1

    def partial_down(h):
        u = jnp.dot(h, wu_ref[...], preferred_element_type=F32)
        u = jnp.square(jnp.maximum(u, 0.0)).astype(BF16)
        return jnp.dot(u, wd_ref[...], preferred_element_type=F32)

    @pl.when(j == 0)
    def _():
        h = _rms(x_ref[...], g1_ref[...]).astype(BF16)
        h_ref[...] = h
        acc_ref[...] = partial_down(h)

    @pl.when((j > 0) & (j < last))
    def _():
        acc_ref[...] += partial_down(h_ref[...])

    @pl.when(j == last)
    def _():
        o_ref[...] = x_ref[...] + _rms(acc_ref[...] + partial_down(h_ref[...]), g2_ref[...])


def _mlp(x, g1, w_up, w_down, g2, *, tm, tf):
    T = x.shape[0]
    assert D_FF // tf >= 2
    return pl.pallas_call(
        _mlp_kernel,
        grid=(T // tm, D_FF // tf),
        in_specs=[
            pl.BlockSpec((tm, D_MODEL), lambda i, j: (i, 0)),
            pl.BlockSpec((1, D_MODEL), lambda i, j: (0, 0)),
            pl.BlockSpec((D_MODEL, tf), lambda i, j: (0, j)),
            pl.BlockSpec((tf, D_MODEL), lambda i, j: (j, 0)),
            pl.BlockSpec((1, D_MODEL), lambda i, j: (0, 0)),
        ],
        out_specs=pl.BlockSpec((tm, D_MODEL), lambda i, j: (i, 0)),
        out_shape=jax.ShapeDtypeStruct((T, D_MODEL), F32),
        scratch_shapes=[pltpu.VMEM((tm, D_MODEL), BF16), pltpu.VMEM((tm, D_MODEL), F32)],
        compiler_params=_params("parallel", "arbitrary"),
        name="mlp",
    )(x, g1, w_up, w_down, g2)


def _pick(n, pref):
    t = min(n, pref)
    while n % t:
        t //= 2
    return t


def _tiles(B, L):
    T = B * L
    tk = _pick(L, 2 * MXU_WIDTH)
    return dict(
        in_proj=dict(tm=_pick(T, 4 * MXU_WIDTH), tn=4 * MXU_WIDTH),
        mla_prep=dict(tm=_pick(L, 2 * MXU_WIDTH)),
        attention=dict(tq=MXU_WIDTH, tk=tk, groups=_pick(L // MXU_WIDTH, 8),
                       unroll=2 if (L // tk) % 4 == 0 else 1),
        gla=dict(cb=_pick(L, 16 * GLA_CHUNK)),
        out_proj=dict(tm=_pick(T, 2 * MXU_WIDTH)),
        mlp=dict(tm=_pick(T, 2 * MXU_WIDTH), tf=4 * MXU_WIDTH),
    )


def _rope_tables(L):
    inv = 1.0 / (ROPE_THETA ** (jnp.arange(0, QK_ROPE, 2, dtype=F32) / QK_ROPE))
    ang = jnp.arange(L, dtype=F32)[:, None] * inv[None, :]
    return jnp.cos(ang), jnp.sin(ang)


def _prepare_weights(w_in, q_a_norm, w_q_b, kv_a_norm, w_kv_b, w_gk_f, b_gk_f, w_gk_b, b_gk_b,
                     gla_norm, w_out, pre_mix_norm, post_mix_norm, pre_mlp_norm, post_mlp_norm,
                     w_up, w_down):
    s = [0]
    for width in (Q_LORA, KV_LORA, QK_ROPE, GLA_QK_WIDTH, GLA_QK_WIDTH, GLA_V_WIDTH,
                  GLA_GATE_RANK, GLA_GATE_RANK, GLA_V_WIDTH):
        s.append(s[-1] + width)
    col = lambda i: w_in[:, s[i]:s[i + 1]]
    w_main = jnp.concatenate([col(0), col(1), col(3), col(4), col(5), col(8)], axis=1).astype(BF16)
    pad = jnp.zeros((D_MODEL, TAIL_WIDTH - QK_ROPE - 2 * GLA_GATE_RANK), w_in.dtype)
    w_tail = jnp.concatenate([col(2), col(6), col(7), pad], axis=1).astype(BF16)
    w_kv = w_kv_b.reshape(KV_LORA, MLA_HEADS, QK_NOPE + V_HEAD)
    row = lambda v: v.reshape(1, -1).astype(F32)
    return dict(
        w_main=w_main, w_tail=w_tail,
        q_norm=row(q_a_norm), kv_norm=row(kv_a_norm),
        wqT=w_q_b.T.astype(BF16),
        wk=w_kv[:, :, :QK_NOPE].reshape(KV_LORA, MLA_HEADS * QK_NOPE).astype(BF16),
        wvT=w_kv[:, :, QK_NOPE:].reshape(KV_LORA, MLA_HEADS * V_HEAD).T.astype(BF16),
        wgf=w_gk_f.astype(F32), bgf=row(b_gk_f), wgb=w_gk_b.astype(F32), bgb=row(b_gk_b),
        gla_norm=row(gla_norm),
        w_out_a=w_out[:MLA_WIDTH].astype(BF16), w_out_b=w_out[MLA_WIDTH:].astype(BF16),
        pre_mix=row(pre_mix_norm), post_mix=row(post_mix_norm),
        pre_mlp=row(pre_mlp_norm), post_mlp=row(post_mlp_norm),
        w_up=w_up.astype(BF16), w_down=w_down.astype(BF16),
    )


def _layer(x, w):
    B, L, _ = x.shape
    T = B * L
    x2 = x.reshape(T, D_MODEL)
    cos, sin = _rope_tables(L)

    t = _tiles(B, L)

    main, tail = _in_proj(x2, w["pre_mix"], w["w_main"], w["w_tail"], **t["in_proj"])
    main = main.reshape(B, L, MAIN_WIDTH)
    tail = tail.reshape(B, L, TAIL_WIDTH)

    qT, k, vT, lgf, lgb = _mla_prep(
        main, tail, w["q_norm"], w["kv_norm"], w["wqT"], w["wk"], w["wvT"],
        w["wgf"], w["bgf"], w["wgb"], w["bgb"], cos, sin, cos.T, sin.T, **t["mla_prep"])
    mla = _attention(qT, k, vT, **t["attention"])

    o_fwd = _gla(main, lgf, None, None, reverse=False, **t["gla"])
    gla = _gla(main, lgb, o_fwd, w["gla_norm"], reverse=True, **t["gla"])

    x2 = _out_proj(x2, mla.reshape(T, MLA_WIDTH), gla.reshape(T, GLA_V_WIDTH),
                   w["w_out_a"], w["w_out_b"], w["post_mix"], **t["out_proj"])
    x2 = _mlp(x2, w["pre_mlp"], w["w_up"], w["w_down"], w["post_mlp"], **t["mlp"])
    return x2.reshape(B, L, D_MODEL)


def kernel(x_prompt, x_sample, w_in, q_a_norm, w_q_b, kv_a_norm, w_kv_b, w_gk_f, b_gk_f, w_gk_b, b_gk_b, gla_norm, w_out, pre_mix_norm, post_mix_norm, pre_mlp_norm, post_mlp_norm, w_up, w_down):
    depth = w_in.shape[0]
    stacked = (w_in, q_a_norm, w_q_b, kv_a_norm, w_kv_b, w_gk_f, b_gk_f, w_gk_b, b_gk_b, gla_norm,
               w_out, pre_mix_norm, post_mix_norm, pre_mlp_norm, post_mlp_norm, w_up, w_down)
    layers = [_prepare_weights(*(t[l] for t in stacked)) for l in range(depth)]
    outs = []
    for x in (x_prompt, x_sample):
        for w in layers:
            x = _layer(x, w)
        outs.append(x)
    return tuple(outs)
```

```python
import functools
import math

import jax
import jax.numpy as jnp
from jax import lax
from jax.experimental import pallas as pl
from jax.experimental.pallas import tpu as pltpu

F32 = jnp.float32
BF16 = jnp.bfloat16

D_MODEL = 2048
MLA_HEADS = 8
QK_NOPE = 128
QK_ROPE = 64
QK_DIM = QK_NOPE + QK_ROPE
V_HEAD = 128
V_AUG = V_HEAD + 16
Q_LORA = 512
KV_LORA = 512
ROPE_THETA = 10000.0
GLA_HEADS = 4
GLA_DK = 128
GLA_DV = 256
GLA_GATE_RANK = 16
GLA_GATE_NORM = 16.0
GLA_CHUNK = 64
D_FF = 4 * D_MODEL
EPS = 1e-6

MLA_WIDTH = MLA_HEADS * V_HEAD
GLA_QK_WIDTH = GLA_HEADS * GLA_DK
GLA_V_WIDTH = GLA_HEADS * GLA_DV
MAIN_WIDTH = Q_LORA + KV_LORA + 2 * GLA_QK_WIDTH + 2 * GLA_V_WIDTH
TAIL_WIDTH = 128
HALF_ROPE = QK_ROPE // 2

Q_PRESCALE = (QK_DIM ** -0.5) * math.log2(math.e)

VMEM_LIMIT = 56 * 1024 * 1024
LANE = 128
MXU_WIDTH = 256

NT_DIMS = (((1,), (1,)), ((), ()))


def _params(*sem):
    return pltpu.CompilerParams(dimension_semantics=sem, vmem_limit_bytes=VMEM_LIMIT)


def _rms(x, gain):
    return x * lax.rsqrt(jnp.mean(x * x, axis=-1, keepdims=True) + EPS) * gain


def _in_proj_kernel(x_ref, g_ref, wm_ref, wt_ref, om_ref, ot_ref, h_ref):
    j = pl.program_id(1)

    @pl.when(j == 0)
    def _():
        h = _rms(x_ref[...], g_ref[...]).astype(BF16)
        h_ref[...] = h
        om_ref[...] = jnp.dot(h, wm_ref[...], preferred_element_type=F32).astype(BF16)
        ot_ref[...] = jnp.dot(h, wt_ref[...], preferred_element_type=F32)

    @pl.when(j != 0)
    def _():
        om_ref[...] = jnp.dot(h_ref[...], wm_ref[...], preferred_element_type=F32).astype(BF16)


def _in_proj(x, gain, w_main, w_tail, *, tm, tn):
    T = x.shape[0]
    return pl.pallas_call(
        _in_proj_kernel,
        grid=(T // tm, MAIN_WIDTH // tn),
        in_specs=[
            pl.BlockSpec((tm, D_MODEL), lambda i, j: (i, 0)),
            pl.BlockSpec((1, D_MODEL), lambda i, j: (0, 0)),
            pl.BlockSpec((D_MODEL, tn), lambda i, j: (0, j)),
            pl.BlockSpec((D_MODEL, TAIL_WIDTH), lambda i, j: (0, 0)),
        ],
        out_specs=[
            pl.BlockSpec((tm, tn), lambda i, j: (i, j)),
            pl.BlockSpec((tm, TAIL_WIDTH), lambda i, j: (i, 0)),
        ],
        out_shape=[
            jax.ShapeDtypeStruct((T, MAIN_WIDTH), BF16),
            jax.ShapeDtypeStruct((T, TAIL_WIDTH), F32),
        ],
        scratch_shapes=[pltpu.VMEM((tm, D_MODEL), BF16)],
        compiler_params=_params("parallel", "arbitrary"),
        name="in_proj",
    )(x, gain, w_main, w_tail)


def _log_sigmoid(z):
    return jnp.minimum(z, 0.0) - jnp.log(1.0 + jnp.exp(-jnp.abs(z)))


def _dot_bf16x3(a, b):
    a_hi = a.astype(BF16)
    a_lo = (a - a_hi.astype(F32)).astype(BF16)
    b_hi = b.astype(BF16)
    b_lo = (b - b_hi.astype(F32)).astype(BF16)
    return (jnp.dot(a_hi, b_hi, preferred_element_type=F32)
            + jnp.dot(a_hi, b_lo, preferred_element_type=F32)
            + jnp.dot(a_lo, b_hi, preferred_element_type=F32))


def _mla_prep_kernel(c_ref, t_ref, qn_ref, kvn_ref, wqT_ref, wk_ref, wvT_ref,
                     wgf_ref, bgf_ref, wgb_ref, bgb_ref, cos_ref, sin_ref, cosT_ref, sinT_ref,
                     qT_ref, k_ref, vT_ref, lgf_ref, lgb_ref):
    c = c_ref[0].astype(F32)
    cq = _rms(c[:, :Q_LORA], qn_ref[...]).astype(BF16)
    ckv = _rms(c[:, Q_LORA:], kvn_ref[...]).astype(BF16)

    qT = lax.dot_general(wqT_ref[...], cq, NT_DIMS, preferred_element_type=F32)
    cosT = cosT_ref[...]
    sinT = sinT_ref[...]
    for h in range(MLA_HEADS):
        r0 = h * QK_DIM
        x1 = qT[r0 + QK_NOPE:r0 + QK_NOPE + HALF_ROPE]
        x2 = qT[r0 + QK_NOPE + HALF_ROPE:r0 + QK_DIM]
        qT_ref[0, h, 0:QK_NOPE, :] = (qT[r0:r0 + QK_NOPE] * Q_PRESCALE).astype(BF16)
        qT_ref[0, h, QK_NOPE:QK_NOPE + HALF_ROPE, :] = ((x1 * cosT - x2 * sinT) * Q_PRESCALE).astype(BF16)
        qT_ref[0, h, QK_NOPE + HALF_ROPE:QK_DIM, :] = ((x2 * cosT + x1 * sinT) * Q_PRESCALE).astype(BF16)

    t = t_ref[0]
    k1 = t[:, 0:HALF_ROPE]
    k2 = t[:, HALF_ROPE:QK_ROPE]
    cos = cos_ref[...]
    sin = sin_ref[...]
    k_pe = jnp.concatenate([k1 * cos - k2 * sin, k2 * cos + k1 * sin], axis=-1).astype(BF16)
    kn = jnp.dot(ckv, wk_ref[...], preferred_element_type=F32)
    for h in range(MLA_HEADS):
        k_ref[0, h, :, 0:QK_NOPE] = kn[:, h * QK_NOPE:(h + 1) * QK_NOPE].astype(BF16)
        k_ref[0, h, :, QK_NOPE:QK_DIM] = k_pe

    vT = lax.dot_general(wvT_ref[...], ckv, NT_DIMS, preferred_element_type=F32)
    for h in range(MLA_HEADS):
        vT_ref[0, h, 0:V_HEAD, :] = vT[h * V_HEAD:(h + 1) * V_HEAD].astype(BF16)
        vT_ref[0, h, V_HEAD:V_AUG, :] = jnp.ones((V_AUG - V_HEAD, vT.shape[1]), BF16)

    gf = t[:, QK_ROPE:QK_ROPE + GLA_GATE_RANK]
    gb = t[:, QK_ROPE + GLA_GATE_RANK:QK_ROPE + 2 * GLA_GATE_RANK]
    zf = _dot_bf16x3(gf, wgf_ref[...]) + bgf_ref[...]
    zb = _dot_bf16x3(gb, wgb_ref[...]) + bgb_ref[...]
    lgf_ref[0] = _log_sigmoid(zf) * (1.0 / GLA_GATE_NORM)
    lgb_ref[0] = _log_sigmoid(zb) * (1.0 / GLA_GATE_NORM)


def _mla_prep(main, tail, q_norm, kv_norm, wqT, wk, wvT, wgf, bgf, wgb, bgb, cos, sin, cosT, sinT, *, tm):
    B, L, _ = main.shape
    const = lambda shape: pl.BlockSpec(shape, lambda b, i: (0,) * len(shape))
    return pl.pallas_call(
        _mla_prep_kernel,
        grid=(B, L // tm),
        in_specs=[
            pl.BlockSpec((1, tm, Q_LORA + KV_LORA), lambda b, i: (b, i, 0)),
            pl.BlockSpec((1, tm, TAIL_WIDTH), lambda b, i: (b, i, 0)),
            const((1, Q_LORA)),
            const((1, KV_LORA)),
            const((MLA_HEADS * QK_DIM, Q_LORA)),
            const((KV_LORA, MLA_HEADS * QK_NOPE)),
            const((MLA_HEADS * V_HEAD, KV_LORA)),
            const((GLA_GATE_RANK, GLA_QK_WIDTH)),
            const((1, GLA_QK_WIDTH)),
            const((GLA_GATE_RANK, GLA_QK_WIDTH)),
            const((1, GLA_QK_WIDTH)),
            pl.BlockSpec((tm, HALF_ROPE), lambda b, i: (i, 0)),
            pl.BlockSpec((tm, HALF_ROPE), lambda b, i: (i, 0)),
            pl.BlockSpec((HALF_ROPE, tm), lambda b, i: (0, i)),
            pl.BlockSpec((HALF_ROPE, tm), lambda b, i: (0, i)),
        ],
        out_specs=[
            pl.BlockSpec((1, MLA_HEADS, QK_DIM, tm), lambda b, i: (b, 0, 0, i)),
            pl.BlockSpec((1, MLA_HEADS, tm, QK_DIM), lambda b, i: (b, 0, i, 0)),
            pl.BlockSpec((1, MLA_HEADS, V_AUG, tm), lambda b, i: (b, 0, 0, i)),
            pl.BlockSpec((1, tm, GLA_QK_WIDTH), lambda b, i: (b, i, 0)),
            pl.BlockSpec((1, tm, GLA_QK_WIDTH), lambda b, i: (b, i, 0)),
        ],
        out_shape=[
            jax.ShapeDtypeStruct((B, MLA_HEADS, QK_DIM, L), BF16),
            jax.ShapeDtypeStruct((B, MLA_HEADS, L, QK_DIM), BF16),
            jax.ShapeDtypeStruct((B, MLA_HEADS, V_AUG, L), BF16),
            jax.ShapeDtypeStruct((B, L, GLA_QK_WIDTH), F32),
            jax.ShapeDtypeStruct((B, L, GLA_QK_WIDTH), F32),
        ],
        compiler_params=_params("parallel", "parallel"),
        name="mla_prep",
    )(main, tail, q_norm, kv_norm, wqT, wk, wvT, wgf, bgf, wgb, bgb, cos, sin, cosT, sinT)


def _attention_kernel(qT_ref, k_ref, vT_ref, o_ref, *scratch, tq, tk, nk, groups, nsteps, unroll):
    tb = groups * tq
    per_slot = lambda refs: (refs[:groups], refs[groups:])
    s_ref = per_slot(scratch[0:2 * groups])
    p_ref = per_slot(scratch[2 * groups:4 * groups])
    alpha_ref = per_slot(scratch[4 * groups:6 * groups])
    m_ref = scratch[6 * groups:7 * groups]
    bmax_ref = per_slot(scratch[7 * groups:9 * groups])
    acc_ref = [scratch[(9 + a) * groups:(10 + a) * groups] for a in range(1 + unroll)]

    def scores(t, g, slot):
        koff = pl.multiple_of((t % nk) * tk, tk)
        qoff = pl.multiple_of((t // nk) * tb + g * tq, tq)
        s = jnp.dot(k_ref[0, 0, pl.ds(koff, tk), :], qT_ref[0, 0, :, pl.ds(qoff, tq)],
                    preferred_element_type=F32)
        s_ref[slot][g][...] = s
        bmax_ref[slot][g][...] = jnp.max(s, axis=0, keepdims=True)

    def softmax(t, g, slot):
        m_old = jnp.where(t % nk == 0, -jnp.inf, m_ref[g][...])
        m_new = jnp.maximum(m_old, bmax_ref[slot][g][...])
        p_ref[slot][g][...] = jnp.exp2(s_ref[slot][g][...] - m_new).astype(BF16)
        alpha_ref[slot][g][...] = jnp.exp2(m_old - m_new)
        m_ref[g][...] = m_new

    def pv(t, g, slot, src, dst):
        koff = pl.multiple_of((t % nk) * tk, tk)
        upd = jnp.dot(vT_ref[0, 0, :, pl.ds(koff, tk)], p_ref[slot][g][...], preferred_element_type=F32)
        acc_ref[dst][g][...] = alpha_ref[slot][g][...] * acc_ref[src][g][...] + upd

    def emit(group):
        for g in range(groups):
            acc = acc_ref[1][g][...]
            qoff = pl.multiple_of(group * tb + g * tq, tq)
            o = acc[:V_HEAD] / acc[V_HEAD:V_HEAD + 1]
            o_ref[0, pl.ds(qoff, tq), :] = o.T.astype(BF16)

    for g in range(groups):
        p_ref[1][g][...] = jnp.ones((tk, tq), BF16)
        alpha_ref[1][g][...] = jnp.zeros((1, tq), F32)
        m_ref[g][...] = jnp.full((1, tq), -jnp.inf, F32)
        acc_ref[0][g][...] = jnp.zeros((V_AUG, tq), F32)
        scores(0, g, 0)

    def body(i, _):
        for h in range(unroll):
            t0 = 2 * (unroll * i + h)
            for g in range(groups):
                scores(t0 + 1, g, 1)
                pv(jnp.maximum(t0 - 1, 0), g, 1, 0, 1 + h)
                softmax(t0, g, 0)
            for g in range(groups):
                scores(jnp.minimum(t0 + 2, nsteps - 1), g, 0)
                pv(t0, g, 0, 1 + h, 0)
                softmax(t0 + 1, g, 1)

        t0 = 2 * unroll * i

        @pl.when((t0 % nk == 0) & (i > 0))
        def _():
            emit(t0 // nk - 1)

        return 0

    lax.fori_loop(0, nsteps // (2 * unroll), body, 0)
    for g in range(groups):
        pv(nsteps - 1, g, 1, 0, 1)
    emit(nsteps // nk - 1)


def _attention(qT, k, vT, *, tq, tk, groups, unroll):
    B, H, _, L = qT.shape
    nk = L // tk
    nsteps = (L // (tq * groups)) * nk
    assert nk % (2 * unroll) == 0
    lanes = lambda n: -(-n // LANE) * LANE
    in_bytes = 2 * (QK_DIM * L + L * lanes(QK_DIM) + V_AUG * L)
    out_bytes = 2 * 2 * L * V_HEAD
    scratch_bytes = groups * (2 * tk * tq * (4 + 2) + (1 + unroll) * V_AUG * tq * 4)
    fits = 2 * in_bytes + out_bytes + scratch_bytes <= (VMEM_LIMIT * 3) // 4
    mode = {} if fits else dict(pipeline_mode=pl.Buffered(1))
    return pl.pallas_call(
        functools.partial(_attention_kernel, tq=tq, tk=tk, nk=nk, groups=groups, nsteps=nsteps,
                          unroll=unroll),
        grid=(B, H),
        in_specs=[
            pl.BlockSpec((1, 1, QK_DIM, L), lambda b, h: (b, h, 0, 0), **mode),
            pl.BlockSpec((1, 1, L, QK_DIM), lambda b, h: (b, h, 0, 0), **mode),
            pl.BlockSpec((1, 1, V_AUG, L), lambda b, h: (b, h, 0, 0), **mode),
        ],
        out_specs=pl.BlockSpec((1, L, V_HEAD), lambda b, h: (b, 0, h)),
        out_shape=jax.ShapeDtypeStruct((B, L, H * V_HEAD), BF16),
        scratch_shapes=(
            [pltpu.VMEM((tk, tq), F32)] * (2 * groups)
            + [pltpu.VMEM((tk, tq), BF16)] * (2 * groups)
            + [pltpu.VMEM((1, tq), F32)] * (2 * groups)
            + [pltpu.VMEM((1, tq), F32)] * groups
            + [pltpu.VMEM((1, tq), F32)] * (2 * groups)
            + [pltpu.VMEM((V_AUG, tq), F32)] * ((1 + unroll) * groups)
        ),
        compiler_params=_params("parallel", "arbitrary"),
        name="attention",
    )(qT, k, vT)


def _gla_kernel(*refs, reverse, final, nchunks):
    if final:
        q_ref, k_ref, v_ref, lg_ref, of_ref, og_ref, gn_ref, o_ref, st_ref = refs
    else:
        q_ref, k_ref, v_ref, lg_ref, o_ref, st_ref = refs
    C = GLA_CHUNK

    @pl.when(pl.program_id(1) == 0)
    def _():
        st_ref[...] = jnp.zeros_like(st_ref)

    row = lax.broadcasted_iota(jnp.int32, (C, C), 0)
    col = lax.broadcasted_iota(jnp.int32, (C, C), 1)
    keep = (col >= row) if reverse else (col <= row)
    tri = keep.astype(BF16)
    last = 0 if reverse else C - 1
    order = [(nchunks - 1 - cc) if reverse else cc for cc in range(nchunks)]

    def gates(c):
        rows = slice(c * C, (c + 1) * C)
        g = lg_ref[0, rows, :]
        g_hi = g.astype(BF16)
        r1 = g - g_hi.astype(F32)
        g_mid = r1.astype(BF16)
        g_lo = (r1 - g_mid.astype(F32)).astype(BF16)
        b = (jnp.dot(tri, g_hi, preferred_element_type=F32)
             + jnp.dot(tri, g_mid, preferred_element_type=F32)
             + jnp.dot(tri, g_lo, preferred_element_type=F32))
        b_last = b[last:last + 1, :]
        kf = k_ref[0, rows, :].astype(F32)
        return dict(
            rows=rows,
            q_dec=(q_ref[0, rows, :].astype(F32) * (jnp.exp(b) * (GLA_DK ** -0.5))).astype(BF16),
            k_inv=(kf * jnp.exp(-b)).astype(BF16),
            k_tail=(kf * jnp.exp(b_last - b)).astype(BF16),
            decay=jnp.exp(b_last),
        )

    def products(ch):
        a, u = [], []
        for h in range(GLA_HEADS):
            ks = slice(h * GLA_DK, (h + 1) * GLA_DK)
            v = v_ref[0, ch["rows"], h * GLA_DV:(h + 1) * GLA_DV]
            a.append(lax.dot_general(ch["q_dec"][:, ks], ch["k_inv"][:, ks], NT_DIMS,
                                     preferred_element_type=F32))
            vT = v.astype(F32).T.astype(BF16)
            u.append(jnp.dot(vT, ch["k_tail"][:, ks], preferred_element_type=F32))
        ch["a"], ch["u"] = a, u

    def outputs(ch):
        rows = ch["rows"]
        for h in range(GLA_HEADS):
            ks = slice(h * GLA_DK, (h + 1) * GLA_DK)
            vs = slice(h * GLA_DV, (h + 1) * GLA_DV)
            a = jnp.where(keep, ch["a"][h], 0.0).astype(BF16)
            st = st_ref[h]
            o = (jnp.dot(a, v_ref[0, rows, vs], preferred_element_type=F32)
                 + lax.dot_general(ch["q_dec"][:, ks], st.astype(BF16), NT_DIMS,
                                   preferred_element_type=F32))
            st_ref[h] = st * ch["decay"][:, ks] + ch["u"][h]
            if final:
                o = o + of_ref[0, rows, vs]
                o = _rms(o, gn_ref[...])
                gate = og_ref[0, rows, vs].astype(F32)
                o = o * (gate * jax.nn.sigmoid(gate))
                o_ref[0, rows, vs] = o.astype(BF16)
            else:
                o_ref[0, rows, vs] = o

    chunks = {}
    for i in range(-2, nchunks):
        if i + 2 < nchunks:
            chunks[i + 2] = gates(order[i + 2])
        if 0 <= i + 1 < nchunks:
            products(chunks[i + 1])
        if i >= 0:
            outputs(chunks.pop(i))


def _gla(main, lg, o_fwd, gla_norm, *, reverse, cb):
    B, L, _ = main.shape
    nb = L // cb
    final = o_fwd is not None
    blk = (lambda i: nb - 1 - i) if reverse else (lambda i: i)
    in_specs = [
        pl.BlockSpec((1, cb, GLA_QK_WIDTH), lambda b, i: (b, blk(i), 2)),
        pl.BlockSpec((1, cb, GLA_QK_WIDTH), lambda b, i: (b, blk(i), 3)),
        pl.BlockSpec((1, cb, GLA_V_WIDTH), lambda b, i: (b, blk(i), 2)),
        pl.BlockSpec((1, cb, GLA_QK_WIDTH), lambda b, i: (b, blk(i), 0)),
    ]
    args = [main, main, main, lg]
    if final:
        in_specs += [
            pl.BlockSpec((1, cb, GLA_V_WIDTH), lambda b, i: (b, blk(i), 0)),
            pl.BlockSpec((1, cb, GLA_V_WIDTH), lambda b, i: (b, blk(i), 3)),
            pl.BlockSpec((1, GLA_DV), lambda b, i: (0, 0)),
        ]
        args += [o_fwd, main, gla_norm]
    return pl.pallas_call(
        functools.partial(_gla_kernel, reverse=reverse, final=final, nchunks=cb // GLA_CHUNK),
        grid=(B, nb),
        in_specs=in_specs,
        out_specs=pl.BlockSpec((1, cb, GLA_V_WIDTH), lambda b, i: (b, blk(i), 0)),
        out_shape=jax.ShapeDtypeStruct((B, L, GLA_V_WIDTH), BF16 if final else F32),
        scratch_shapes=[pltpu.VMEM((GLA_HEADS, GLA_DV, GLA_DK), F32)],
        compiler_params=_params("parallel", "arbitrary"),
        name="gla_bwd" if reverse else "gla_fwd",
    )(*args)


def _out_proj_kernel(x_ref, mla_ref, gla_ref, wa_ref, wb_ref, g_ref, o_ref):
    mix = (jnp.dot(mla_ref[...], wa_ref[...], preferred_element_type=F32)
           + jnp.dot(gla_ref[...], wb_ref[...], preferred_element_type=F32))
    o_ref[...] = x_ref[...] + _rms(mix, g_ref[...])


def _out_proj(x, mla, gla, w_a, w_b, gain, *, tm):
    T = x.shape[0]
    return pl.pallas_call(
        _out_proj_kernel,
        grid=(T // tm,),
        in_specs=[
            pl.BlockSpec((tm, D_MODEL), lambda i: (i, 0)),
            pl.BlockSpec((tm, MLA_WIDTH), lambda i: (i, 0)),
            pl.BlockSpec((tm, GLA_V_WIDTH), lambda i: (i, 0)),
            pl.BlockSpec((MLA_WIDTH, D_MODEL), lambda i: (0, 0)),
            pl.BlockSpec((GLA_V_WIDTH, D_MODEL), lambda i: (0, 0)),
            pl.BlockSpec((1, D_MODEL), lambda i: (0, 0)),
        ],
        out_specs=pl.BlockSpec((tm, D_MODEL), lambda i: (i, 0)),
        out_shape=jax.ShapeDtypeStruct((T, D_MODEL), F32),
        compiler_params=_params("parallel"),
        name="out_proj",
    )(x, mla, gla, w_a, w_b, gain)


def _mlp_kernel(x_ref, g1_ref, wu_ref, wd_ref, g2_ref, o_ref, h_ref, acc_ref):
    j = pl.program_id(1)
    last = pl.num_programs(1) - 1

    def partial_down(h):
        u = jnp.dot(h, wu_ref[...], preferred_element_type=F32)
        u = jnp.square(jnp.maximum(u, 0.0)).astype(BF16)
        return jnp.dot(u, wd_ref[...], preferred_element_type=F32)

    @pl.when(j == 0)
    def _():
        h = _rms(x_ref[...], g1_ref[...]).astype(BF16)
        h_ref[...] = h
        acc_ref[...] = partial_down(h)

    @pl.when((j > 0) & (j < last))
    def _():
        acc_ref[...] += partial_down(h_ref[...])

    @pl.when(j == last)
    def _():
        o_ref[...] = x_ref[...] + _rms(acc_ref[...] + partial_down(h_ref[...]), g2_ref[...])


def _mlp(x, g1, w_up, w_down, g2, *, tm, tf):
    T = x.shape[0]
    assert D_FF // tf >= 2
    return pl.pallas_call(
        _mlp_kernel,
        grid=(T // tm, D_FF // tf),
        in_specs=[
            pl.BlockSpec((tm, D_MODEL), lambda i, j: (i, 0)),
            pl.BlockSpec((1, D_MODEL), lambda i, j: (0, 0)),
            pl.BlockSpec((D_MODEL, tf), lambda i, j: (0, j)),
            pl.BlockSpec((tf, D_MODEL), lambda i, j: (j, 0)),
            pl.BlockSpec((1, D_MODEL), lambda i, j: (0, 0)),
        ],
        out_specs=pl.BlockSpec((tm, D_MODEL), lambda i, j: (i, 0)),
        out_shape=jax.ShapeDtypeStruct((T, D_MODEL), F32),
        scratch_shapes=[pltpu.VMEM((tm, D_MODEL), BF16), pltpu.VMEM((tm, D_MODEL), F32)],
        compiler_params=_params("parallel", "arbitrary"),
        name="mlp",
    )(x, g1, w_up, w_down, g2)


def _pick(n, pref):
    t = min(n, pref)
    while n % t:
        t //= 2
    return t


def _tiles(B, L):
    T = B * L
    tk = _pick(L, 2 * MXU_WIDTH)
    return dict(
        in_proj=dict(tm=_pick(T, 2 * MXU_WIDTH), tn=MAIN_WIDTH),
        mla_prep=dict(tm=_pick(L, 4 * MXU_WIDTH)),
        attention=dict(tq=MXU_WIDTH, tk=tk, groups=_pick(L // MXU_WIDTH, 8),
                       unroll=2 if (L // tk) % 4 == 0 else 1),
        gla=dict(cb=_pick(L, 16 * GLA_CHUNK)),
        out_proj=dict(tm=_pick(T, 2 * MXU_WIDTH)),
        mlp=dict(tm=_pick(T, 2 * MXU_WIDTH), tf=4 * MXU_WIDTH),
    )


def _rope_tables(L):
    inv = 1.0 / (ROPE_THETA ** (jnp.arange(0, QK_ROPE, 2, dtype=F32) / QK_ROPE))
    ang = jnp.arange(L, dtype=F32)[:, None] * inv[None, :]
    return jnp.cos(ang), jnp.sin(ang)


def _prepare_weights(w_in, q_a_norm, w_q_b, kv_a_norm, w_kv_b, w_gk_f, b_gk_f, w_gk_b, b_gk_b,
                     gla_norm, w_out, pre_mix_norm, post_mix_norm, pre_mlp_norm, post_mlp_norm,
                     w_up, w_down):
    s = [0]
    for width in (Q_LORA, KV_LORA, QK_ROPE, GLA_QK_WIDTH, GLA_QK_WIDTH, GLA_V_WIDTH,
                  GLA_GATE_RANK, GLA_GATE_RANK, GLA_V_WIDTH):
        s.append(s[-1] + width)
    col = lambda i: w_in[:, s[i]:s[i + 1]]
    w_main = jnp.concatenate([col(0), col(1), col(3), col(4), col(5), col(8)], axis=1).astype(BF16)
    pad = jnp.zeros((D_MODEL, TAIL_WIDTH - QK_ROPE - 2 * GLA_GATE_RANK), w_in.dtype)
    w_tail = jnp.concatenate([col(2), col(6), col(7), pad], axis=1).astype(BF16)
    w_kv = w_kv_b.reshape(KV_LORA, MLA_HEADS, QK_NOPE + V_HEAD)
    row = lambda v: v.reshape(1, -1).astype(F32)
    return dict(
        w_main=w_main, w_tail=w_tail,
        q_norm=row(q_a_norm), kv_norm=row(kv_a_norm),
        wqT=w_q_b.T.astype(BF16),
        wk=w_kv[:, :, :QK_NOPE].reshape(KV_LORA, MLA_HEADS * QK_NOPE).astype(BF16),
        wvT=w_kv[:, :, QK_NOPE:].reshape(KV_LORA, MLA_HEADS * V_HEAD).T.astype(BF16),
        wgf=w_gk_f.astype(F32), bgf=row(b_gk_f), wgb=w_gk_b.astype(F32), bgb=row(b_gk_b),
        gla_norm=row(gla_norm),
        w_out_a=w_out[:MLA_WIDTH].astype(BF16), w_out_b=w_out[MLA_WIDTH:].astype(BF16),
        pre_mix=row(pre_mix_norm), post_mix=row(post_mix_norm),
        pre_mlp=row(pre_mlp_norm), post_mlp=row(post_mlp_norm),
        w_up=w_up.astype(BF16), w_down=w_down.astype(BF16),
    )


def _layer(x, w):
    B, L, _ = x.shape
    T = B * L
    x2 = x.reshape(T, D_MODEL)
    cos, sin = _rope_tables(L)

    t = _tiles(B, L)

    main, tail = _in_proj(x2, w["pre_mix"], w["w_main"], w["w_tail"], **t["in_proj"])
    main = main.reshape(B, L, MAIN_WIDTH)
    tail = tail.reshape(B, L, TAIL_WIDTH)

    qT, k, vT, lgf, lgb = _mla_prep(
        main, tail, w["q_norm"], w["kv_norm"], w["wqT"], w["wk"], w["wvT"],
        w["wgf"], w["bgf"], w["wgb"], w["bgb"], cos, sin, cos.T, sin.T, **t["mla_prep"])
    mla = _attention(qT, k, vT, **t["attention"])

    o_fwd = _gla(main, lgf, None, None, reverse=False, **t["gla"])
    gla = _gla(main, lgb, o_fwd, w["gla_norm"], reverse=True, **t["gla"])

    x2 = _out_proj(x2, mla.reshape(T, MLA_WIDTH), gla.reshape(T, GLA_V_WIDTH),
                   w["w_out_a"], w["w_out_b"], w["post_mix"], **t["out_proj"])
    x2 = _mlp(x2, w["pre_mlp"], w["w_up"], w["w_down"], w["post_mlp"], **t["mlp"])
    return x2.reshape(B, L, D_MODEL)


def kernel(x_prompt, x_sample, w_in, q_a_norm, w_q_b, kv_a_norm, w_kv_b, w_gk_f, b_gk_f, w_gk_b, b_gk_b, gla_norm, w_out, pre_mix_norm, post_mix_norm, pre_mlp_norm, post_mlp_norm, w_up, w_down):
    depth = w_in.shape[0]
    stacked = (w_in, q_a_norm, w_q_b, kv_a_norm, w_kv_b, w_gk_f, b_gk_f, w_gk_b, b_gk_b, gla_norm,
               w_out, pre_mix_norm, post_mix_norm, pre_mlp_norm, post_mlp_norm, w_up, w_down)
    layers = [_prepare_weights(*(t[l] for t in stacked)) for l in range(depth)]
    outs = []
    for x in (x_prompt, x_sample):
        for w in layers:
            x = _layer(x, w)
        outs.append(x)
    return tuple(outs)
```

```python
import functools
import math

import jax
import jax.numpy as jnp
from jax import lax
from jax.experimental import pallas as pl
from jax.experimental.pallas import tpu as pltpu

F32 = jnp.float32
BF16 = jnp.bfloat16

D_MODEL = 2048
MLA_HEADS = 8
QK_NOPE = 128
QK_ROPE = 64
QK_DIM = QK_NOPE + QK_ROPE
V_HEAD = 128
V_AUG = V_HEAD + 16
Q_LORA = 512
KV_LORA = 512
ROPE_THETA = 10000.0
GLA_HEADS = 4
GLA_DK = 128
GLA_DV = 256
GLA_GATE_RANK = 16
GLA_GATE_NORM = 16.0
GLA_CHUNK = 64
D_FF = 4 * D_MODEL
EPS = 1e-6

MLA_WIDTH = MLA_HEADS * V_HEAD
GLA_QK_WIDTH = GLA_HEADS * GLA_DK
GLA_V_WIDTH = GLA_HEADS * GLA_DV
MAIN_WIDTH = Q_LORA + KV_LORA + 2 * GLA_QK_WIDTH + 2 * GLA_V_WIDTH
TAIL_WIDTH = 128
HALF_ROPE = QK_ROPE // 2

Q_PRESCALE = (QK_DIM ** -0.5) * math.log2(math.e)

VMEM_LIMIT = 56 * 1024 * 1024
LANE = 128
MXU_WIDTH = 256

NT_DIMS = (((1,), (1,)), ((), ()))


def _params(*sem):
    return pltpu.CompilerParams(dimension_semantics=sem, vmem_limit_bytes=VMEM_LIMIT)


def _rms(x, gain):
    return x * lax.rsqrt(jnp.mean(x * x, axis=-1, keepdims=True) + EPS) * gain


def _in_proj_kernel(x_ref, g_ref, wm_ref, wt_ref, om_ref, ot_ref, h_ref):
    j = pl.program_id(1)

    @pl.when(j == 0)
    def _():
        h = _rms(x_ref[...], g_ref[...]).astype(BF16)
        h_ref[...] = h
        om_ref[...] = jnp.dot(h, wm_ref[...], preferred_element_type=F32).astype(BF16)
        ot_ref[...] = jnp.dot(h, wt_ref[...], preferred_element_type=F32)

    @pl.when(j != 0)
    def _():
        om_ref[...] = jnp.dot(h_ref[...], wm_ref[...], preferred_element_type=F32).astype(BF16)


def _in_proj(x, gain, w_main, w_tail, *, tm, tn):
    T = x.shape[0]
    return pl.pallas_call(
        _in_proj_kernel,
        grid=(T // tm, MAIN_WIDTH // tn),
        in_specs=[
            pl.BlockSpec((tm, D_MODEL), lambda i, j: (i, 0)),
            pl.BlockSpec((1, D_MODEL), lambda i, j: (0, 0)),
            pl.BlockSpec((D_MODEL, tn), lambda i, j: (0, j)),
            pl.BlockSpec((D_MODEL, TAIL_WIDTH), lambda i, j: (0, 0)),
        ],
        out_specs=[
            pl.BlockSpec((tm, tn), lambda i, j: (i, j)),
            pl.BlockSpec((tm, TAIL_WIDTH), lambda i, j: (i, 0)),
        ],
        out_shape=[
            jax.ShapeDtypeStruct((T, MAIN_WIDTH), BF16),
            jax.ShapeDtypeStruct((T, TAIL_WIDTH), F32),
        ],
        scratch_shapes=[pltpu.VMEM((tm, D_MODEL), BF16)],
        compiler_params=_params("parallel", "arbitrary"),
        name="in_proj",
    )(x, gain, w_main, w_tail)


def _log_sigmoid(z):
    return jnp.minimum(z, 0.0) - jnp.log(1.0 + jnp.exp(-jnp.abs(z)))


def _dot_bf16x3(a, b):
    a_hi = a.astype(BF16)
    a_lo = (a - a_hi.astype(F32)).astype(BF16)
    b_hi = b.astype(BF16)
    b_lo = (b - b_hi.astype(F32)).astype(BF16)
    return (jnp.dot(a_hi, b_hi, preferred_element_type=F32)
            + jnp.dot(a_hi, b_lo, preferred_element_type=F32)
            + jnp.dot(a_lo, b_hi, preferred_element_type=F32))


def _mla_prep_kernel(c_ref, t_ref, qn_ref, kvn_ref, wqT_ref, wk_ref, wvT_ref,
                     wgf_ref, bgf_ref, wgb_ref, bgb_ref, cos_ref, sin_ref, cosT_ref, sinT_ref,
                     qT_ref, k_ref, vT_ref, lgf_ref, lgb_ref):
    c = c_ref[0].astype(F32)
    cq = _rms(c[:, :Q_LORA], qn_ref[...]).astype(BF16)
    ckv = _rms(c[:, Q_LORA:], kvn_ref[...]).astype(BF16)

    qT = lax.dot_general(wqT_ref[...], cq, NT_DIMS, preferred_element_type=F32)
    cosT = cosT_ref[...]
    sinT = sinT_ref[...]
    for h in range(MLA_HEADS):
        r0 = h * QK_DIM
        x1 = qT[r0 + QK_NOPE:r0 + QK_NOPE + HALF_ROPE]
        x2 = qT[r0 + QK_NOPE + HALF_ROPE:r0 + QK_DIM]
        qT_ref[0, h, 0:QK_NOPE, :] = (qT[r0:r0 + QK_NOPE] * Q_PRESCALE).astype(BF16)
        qT_ref[0, h, QK_NOPE:QK_NOPE + HALF_ROPE, :] = ((x1 * cosT - x2 * sinT) * Q_PRESCALE).astype(BF16)
        qT_ref[0, h, QK_NOPE + HALF_ROPE:QK_DIM, :] = ((x2 * cosT + x1 * sinT) * Q_PRESCALE).astype(BF16)

    t = t_ref[0]
    k1 = t[:, 0:HALF_ROPE]
    k2 = t[:, HALF_ROPE:QK_ROPE]
    cos = cos_ref[...]
    sin = sin_ref[...]
    k_pe = jnp.concatenate([k1 * cos - k2 * sin, k2 * cos + k1 * sin], axis=-1).astype(BF16)
    kn = jnp.dot(ckv, wk_ref[...], preferred_element_type=F32)
    for h in range(MLA_HEADS):
        k_ref[0, h, :, 0:QK_NOPE] = kn[:, h * QK_NOPE:(h + 1) * QK_NOPE].astype(BF16)
        k_ref[0, h, :, QK_NOPE:QK_DIM] = k_pe

    vT = lax.dot_general(wvT_ref[...], ckv, NT_DIMS, preferred_element_type=F32)
    for h in range(MLA_HEADS):
        vT_ref[0, h, 0:V_HEAD, :] = vT[h * V_HEAD:(h + 1) * V_HEAD].astype(BF16)
        vT_ref[0, h, V_HEAD:V_AUG, :] = jnp.ones((V_AUG - V_HEAD, vT.shape[1]), BF16)

    gf = t[:, QK_ROPE:QK_ROPE + GLA_GATE_RANK]
    gb = t[:, QK_ROPE + GLA_GATE_RANK:QK_ROPE + 2 * GLA_GATE_RANK]
    zf = _dot_bf16x3(gf, wgf_ref[...]) + bgf_ref[...]
    zb = _dot_bf16x3(gb, wgb_ref[...]) + bgb_ref[...]
    lgf_ref[0] = _log_sigmoid(zf) * (1.0 / GLA_GATE_NORM)
    lgb_ref[0] = _log_sigmoid(zb) * (1.0 / GLA_GATE_NORM)


def _mla_prep(main, tail, q_norm, kv_norm, wqT, wk, wvT, wgf, bgf, wgb, bgb, cos, sin, cosT, sinT, *, tm):
    B, L, _ = main.shape
    const = lambda shape: pl.BlockSpec(shape, lambda b, i: (0,) * len(shape))
    return pl.pallas_call(
        _mla_prep_kernel,
        grid=(B, L // tm),
        in_specs=[
            pl.BlockSpec((1, tm, Q_LORA + KV_LORA), lambda b, i: (b, i, 0)),
            pl.BlockSpec((1, tm, TAIL_WIDTH), lambda b, i: (b, i, 0)),
            const((1, Q_LORA)),
            const((1, KV_LORA)),
            const((MLA_HEADS * QK_DIM, Q_LORA)),
            const((KV_LORA, MLA_HEADS * QK_NOPE)),
            const((MLA_HEADS * V_HEAD, KV_LORA)),
            const((GLA_GATE_RANK, GLA_QK_WIDTH)),
            const((1, GLA_QK_WIDTH)),
            const((GLA_GATE_RANK, GLA_QK_WIDTH)),
            const((1, GLA_QK_WIDTH)),
            pl.BlockSpec((tm, HALF_ROPE), lambda b, i: (i, 0)),
            pl.BlockSpec((tm, HALF_ROPE), lambda b, i: (i, 0)),
            pl.BlockSpec((HALF_ROPE, tm), lambda b, i: (0, i)),
            pl.BlockSpec((HALF_ROPE, tm), lambda b, i: (0, i)),
        ],
        out_specs=[
            pl.BlockSpec((1, MLA_HEADS, QK_DIM, tm), lambda b, i: (b, 0, 0, i)),
            pl.BlockSpec((1, MLA_HEADS, tm, QK_DIM), lambda b, i: (b, 0, i, 0)),
            pl.BlockSpec((1, MLA_HEADS, V_AUG, tm), lambda b, i: (b, 0, 0, i)),
            pl.BlockSpec((1, tm, GLA_QK_WIDTH), lambda b, i: (b, i, 0)),
            pl.BlockSpec((1, tm, GLA_QK_WIDTH), lambda b, i: (b, i, 0)),
        ],
        out_shape=[
            jax.ShapeDtypeStruct((B, MLA_HEADS, QK_DIM, L), BF16),
            jax.ShapeDtypeStruct((B, MLA_HEADS, L, QK_DIM), BF16),
            jax.ShapeDtypeStruct((B, MLA_HEADS, V_AUG, L), BF16),
            jax.ShapeDtypeStruct((B, L, GLA_QK_WIDTH), F32),
            jax.ShapeDtypeStruct((B, L, GLA_QK_WIDTH), F32),
        ],
        compiler_params=_params("parallel", "parallel"),
        name="mla_prep",
    )(main, tail, q_norm, kv_norm, wqT, wk, wvT, wgf, bgf, wgb, bgb, cos, sin, cosT, sinT)


def _attention_kernel(qT_ref, k_ref, vT_ref, o_ref, *scratch, tq, tk, nk, groups, nsteps, unroll):
    tb = groups * tq
    per_slot = lambda refs: (refs[:groups], refs[groups:])
    s_ref = per_slot(scratch[0:2 * groups])
    p_ref = per_slot(scratch[2 * groups:4 * groups])
    alpha_ref = per_slot(scratch[4 * groups:6 * groups])
    m_ref = scratch[6 * groups:7 * groups]
    bmax_ref = per_slot(scratch[7 * groups:9 * groups])
    acc_ref = [scratch[(9 + a) * groups:(10 + a) * groups] for a in range(1 + unroll)]

    def scores(t, g, slot):
        koff = pl.multiple_of((t % nk) * tk, tk)
        qoff = pl.multiple_of((t // nk) * tb + g * tq, tq)
        s = jnp.dot(k_ref[0, 0, pl.ds(koff, tk), :], qT_ref[0, 0, :, pl.ds(qoff, tq)],
                    preferred_element_type=F32)
        s_ref[slot][g][...] = s
        bmax_ref[slot][g][...] = jnp.max(s, axis=0, keepdims=True)

    def softmax(t, g, slot):
        m_old = jnp.where(t % nk == 0, -jnp.inf, m_ref[g][...])
        m_new = jnp.maximum(m_old, bmax_ref[slot][g][...])
        p_ref[slot][g][...] = jnp.exp2(s_ref[slot][g][...] - m_new).astype(BF16)
        alpha_ref[slot][g][...] = jnp.exp2(m_old - m_new)
        m_ref[g][...] = m_new

    def pv(t, g, slot, src, dst):
        koff = pl.multiple_of((t % nk) * tk, tk)
        upd = jnp.dot(vT_ref[0, 0, :, pl.ds(koff, tk)], p_ref[slot][g][...], preferred_element_type=F32)
        acc_ref[dst][g][...] = alpha_ref[slot][g][...] * acc_ref[src][g][...] + upd

    def emit(group):
        for g in range(groups):
            acc = acc_ref[1][g][...]
            qoff = pl.multiple_of(group * tb + g * tq, tq)
            o = acc[:V_HEAD] / acc[V_HEAD:V_HEAD + 1]
            o_ref[0, pl.ds(qoff, tq), :] = o.T.astype(BF16)

    for g in range(groups):
        p_ref[1][g][...] = jnp.ones((tk, tq), BF16)
        alpha_ref[1][g][...] = jnp.zeros((1, tq), F32)
        m_ref[g][...] = jnp.full((1, tq), -jnp.inf, F32)
        acc_ref[0][g][...] = jnp.zeros((V_AUG, tq), F32)
        scores(0, g, 0)

    def body(i, _):
        for h in range(unroll):
            t0 = 2 * (unroll * i + h)
            for g in range(groups):
                scores(t0 + 1, g, 1)
                pv(jnp.maximum(t0 - 1, 0), g, 1, 0, 1 + h)
                softmax(t0, g, 0)
            for g in range(groups):
                scores(jnp.minimum(t0 + 2, nsteps - 1), g, 0)
                pv(t0, g, 0, 1 + h, 0)
                softmax(t0 + 1, g, 1)

        t0 = 2 * unroll * i

        @pl.when((t0 % nk == 0) & (i > 0))
        def _():
            emit(t0 // nk - 1)

        return 0

    lax.fori_loop(0, nsteps // (2 * unroll), body, 0)
    for g in range(groups):
        pv(nsteps - 1, g, 1, 0, 1)
    emit(nsteps // nk - 1)


ATTENTION_VMEM_BUDGET = (VMEM_LIMIT * 3) // 4


def _attention_vmem(L, tq, tk, groups, unroll):
    lanes = lambda n: -(-n // LANE) * LANE
    in_bytes = 2 * (QK_DIM * L + L * lanes(QK_DIM) + V_AUG * L)
    out_bytes = 2 * 2 * L * V_HEAD
    scratch_bytes = groups * (2 * tk * tq * (4 + 2) + (1 + unroll) * V_AUG * tq * 4)
    return in_bytes, out_bytes + scratch_bytes


def _attention(qT, k, vT, *, tq, tk, groups, unroll):
    B, H, _, L = qT.shape
    nk = L // tk
    nsteps = (L // (tq * groups)) * nk
    assert nk % (2 * unroll) == 0
    in_bytes, rest_bytes = _attention_vmem(L, tq, tk, groups, unroll)
    fits = 2 * in_bytes + rest_bytes <= ATTENTION_VMEM_BUDGET
    mode = {} if fits else dict(pipeline_mode=pl.Buffered(1))
    return pl.pallas_call(
        functools.partial(_attention_kernel, tq=tq, tk=tk, nk=nk, groups=groups, nsteps=nsteps,
                          unroll=unroll),
        grid=(B, H),
        in_specs=[
            pl.BlockSpec((1, 1, QK_DIM, L), lambda b, h: (b, h, 0, 0), **mode),
            pl.BlockSpec((1, 1, L, QK_DIM), lambda b, h: (b, h, 0, 0), **mode),
            pl.BlockSpec((1, 1, V_AUG, L), lambda b, h: (b, h, 0, 0), **mode),
        ],
        out_specs=pl.BlockSpec((1, L, V_HEAD), lambda b, h: (b, 0, h)),
        out_shape=jax.ShapeDtypeStruct((B, L, H * V_HEAD), BF16),
        scratch_shapes=(
            [pltpu.VMEM((tk, tq), F32)] * (2 * groups)
            + [pltpu.VMEM((tk, tq), BF16)] * (2 * groups)
            + [pltpu.VMEM((1, tq), F32)] * (2 * groups)
            + [pltpu.VMEM((1, tq), F32)] * groups
            + [pltpu.VMEM((1, tq), F32)] * (2 * groups)
            + [pltpu.VMEM((V_AUG, tq), F32)] * ((1 + unroll) * groups)
        ),
        compiler_params=_params("parallel", "arbitrary"),
        name="attention",
    )(qT, k, vT)


def _gla_kernel(*refs, reverse, final, nchunks):
    if final:
        q_ref, k_ref, v_ref, lg_ref, of_ref, og_ref, gn_ref, o_ref, st_ref = refs
    else:
        q_ref, k_ref, v_ref, lg_ref, o_ref, st_ref = refs
    C = GLA_CHUNK

    @pl.when(pl.program_id(1) == 0)
    def _():
        st_ref[...] = jnp.zeros_like(st_ref)

    row = lax.broadcasted_iota(jnp.int32, (C, C), 0)
    col = lax.broadcasted_iota(jnp.int32, (C, C), 1)
    keep = (col >= row) if reverse else (col <= row)
    tri = keep.astype(BF16)
    last = 0 if reverse else C - 1
    order = [(nchunks - 1 - cc) if reverse else cc for cc in range(nchunks)]

    def gates(c):
        rows = slice(c * C, (c + 1) * C)
        g = lg_ref[0, rows, :]
        g_hi = g.astype(BF16)
        r1 = g - g_hi.astype(F32)
        g_mid = r1.astype(BF16)
        g_lo = (r1 - g_mid.astype(F32)).astype(BF16)
        b = (jnp.dot(tri, g_hi, preferred_element_type=F32)
             + jnp.dot(tri, g_mid, preferred_element_type=F32)
             + jnp.dot(tri, g_lo, preferred_element_type=F32))
        b_last = b[last:last + 1, :]
        kf = k_ref[0, rows, :].astype(F32)
        return dict(
            rows=rows,
            q_dec=(q_ref[0, rows, :].astype(F32) * (jnp.exp(b) * (GLA_DK ** -0.5))).astype(BF16),
            k_inv=(kf * jnp.exp(-b)).astype(BF16),
            k_tail=(kf * jnp.exp(b_last - b)).astype(BF16),
            decay=jnp.exp(b_last),
        )

    def products(ch):
        a, u = [], []
        for h in range(GLA_HEADS):
            ks = slice(h * GLA_DK, (h + 1) * GLA_DK)
            v = v_ref[0, ch["rows"], h * GLA_DV:(h + 1) * GLA_DV]
            a.append(lax.dot_general(ch["q_dec"][:, ks], ch["k_inv"][:, ks], NT_DIMS,
                                     preferred_element_type=F32))
            vT = v.astype(F32).T.astype(BF16)
            u.append(jnp.dot(vT, ch["k_tail"][:, ks], preferred_element_type=F32))
        ch["a"], ch["u"] = a, u

    def outputs(ch):
        rows = ch["rows"]
        for h in range(GLA_HEADS):
            ks = slice(h * GLA_DK, (h + 1) * GLA_DK)
            vs = slice(h * GLA_DV, (h + 1) * GLA_DV)
            a = jnp.where(keep, ch["a"][h], 0.0).astype(BF16)
            st = st_ref[h]
            o = (jnp.dot(a, v_ref[0, rows, vs], preferred_element_type=F32)
                 + lax.dot_general(ch["q_dec"][:, ks], st.astype(BF16), NT_DIMS,
                                   preferred_element_type=F32))
            st_ref[h] = st * ch["decay"][:, ks] + ch["u"][h]
            if final:
                o = o + of_ref[0, rows, vs]
                o = _rms(o, gn_ref[...])
                gate = og_ref[0, rows, vs].astype(F32)
                o = o * (gate * jax.nn.sigmoid(gate))
                o_ref[0, rows, vs] = o.astype(BF16)
            else:
                o_ref[0, rows, vs] = o

    chunks = {}
    for i in range(-2, nchunks):
        if i + 2 < nchunks:
            chunks[i + 2] = gates(order[i + 2])
        if 0 <= i + 1 < nchunks:
            products(chunks[i + 1])
        if i >= 0:
            outputs(chunks.pop(i))


def _gla(main, lg, o_fwd, gla_norm, *, reverse, cb):
    B, L, _ = main.shape
    nb = L // cb
    final = o_fwd is not None
    blk = (lambda i: nb - 1 - i) if reverse else (lambda i: i)
    in_specs = [
        pl.BlockSpec((1, cb, GLA_QK_WIDTH), lambda b, i: (b, blk(i), 2)),
        pl.BlockSpec((1, cb, GLA_QK_WIDTH), lambda b, i: (b, blk(i), 3)),
        pl.BlockSpec((1, cb, GLA_V_WIDTH), lambda b, i: (b, blk(i), 2)),
        pl.BlockSpec((1, cb, GLA_QK_WIDTH), lambda b, i: (b, blk(i), 0)),
    ]
    args = [main, main, main, lg]
    if final:
        in_specs += [
            pl.BlockSpec((1, cb, GLA_V_WIDTH), lambda b, i: (b, blk(i), 0)),
            pl.BlockSpec((1, cb, GLA_V_WIDTH), lambda b, i: (b, blk(i), 3)),
            pl.BlockSpec((1, GLA_DV), lambda b, i: (0, 0)),
        ]
        args += [o_fwd, main, gla_norm]
    return pl.pallas_call(
        functools.partial(_gla_kernel, reverse=reverse, final=final, nchunks=cb // GLA_CHUNK),
        grid=(B, nb),
        in_specs=in_specs,
        out_specs=pl.BlockSpec((1, cb, GLA_V_WIDTH), lambda b, i: (b, blk(i), 0)),
        out_shape=jax.ShapeDtypeStruct((B, L, GLA_V_WIDTH), BF16 if final else F32),
        scratch_shapes=[pltpu.VMEM((GLA_HEADS, GLA_DV, GLA_DK), F32)],
        compiler_params=_params("parallel", "arbitrary"),
        name="gla_bwd" if reverse else "gla_fwd",
    )(*args)


def _out_proj_kernel(x_ref, mla_ref, gla_ref, wa_ref, wb_ref, g_ref, o_ref):
    mix = (jnp.dot(mla_ref[...], wa_ref[...], preferred_element_type=F32)
           + jnp.dot(gla_ref[...], wb_ref[...], preferred_element_type=F32))
    o_ref[...] = x_ref[...] + _rms(mix, g_ref[...])


def _out_proj(x, mla, gla, w_a, w_b, gain, *, tm):
    T = x.shape[0]
    return pl.pallas_call(
        _out_proj_kernel,
        grid=(T // tm,),
        in_specs=[
            pl.BlockSpec((tm, D_MODEL), lambda i: (i, 0)),
            pl.BlockSpec((tm, MLA_WIDTH), lambda i: (i, 0)),
            pl.BlockSpec((tm, GLA_V_WIDTH), lambda i: (i, 0)),
            pl.BlockSpec((MLA_WIDTH, D_MODEL), lambda i: (0, 0)),
            pl.BlockSpec((GLA_V_WIDTH, D_MODEL), lambda i: (0, 0)),
            pl.BlockSpec((1, D_MODEL), lambda i: (0, 0)),
        ],
        out_specs=pl.BlockSpec((tm, D_MODEL), lambda i: (i, 0)),
        out_shape=jax.ShapeDtypeStruct((T, D_MODEL), F32),
        compiler_params=_params("parallel"),
        name="out_proj",
    )(x, mla, gla, w_a, w_b, gain)


def _mlp_kernel(x_ref, g1_ref, wu_ref, wd_ref, g2_ref, o_ref, h_ref, acc_ref):
    j = pl.program_id(1)
    last = pl.num_programs(1) - 1

    def partial_down(h):
        u = jnp.dot(h, wu_ref[...], preferred_element_type=F32)
        u = jnp.square(jnp.maximum(u, 0.0)).astype(BF16)
        return jnp.dot(u, wd_ref[...], preferred_element_type=F32)

    @pl.when(j == 0)
    def _():
        h = _rms(x_ref[...], g1_ref[...]).astype(BF16)
        h_ref[...] = h
        acc_ref[...] = partial_down(h)

    @pl.when((j > 0) & (j < last))
    def _():
        acc_ref[...] += partial_down(h_ref[...])

    @pl.when(j == last)
    def _():
        o_ref[...] = x_ref[...] + _rms(acc_ref[...] + partial_down(h_ref[...]), g2_ref[...])


def _mlp(x, g1, w_up, w_down, g2, *, tm, tf):
    T = x.shape[0]
    assert D_FF // tf >= 2
    return pl.pallas_call(
        _mlp_kernel,
        grid=(T // tm, D_FF // tf),
        in_specs=[
            pl.BlockSpec((tm, D_MODEL), lambda i, j: (i, 0)),
            pl.BlockSpec((1, D_MODEL), lambda i, j: (0, 0)),
            pl.BlockSpec((D_MODEL, tf), lambda i, j: (0, j)),
            pl.BlockSpec((tf, D_MODEL), lambda i, j: (j, 0)),
            pl.BlockSpec((1, D_MODEL), lambda i, j: (0, 0)),
        ],
        out_specs=pl.BlockSpec((tm, D_MODEL), lambda i, j: (i, 0)),
        out_shape=jax.ShapeDtypeStruct((T, D_MODEL), F32),
        scratch_shapes=[pltpu.VMEM((tm, D_MODEL), BF16), pltpu.VMEM((tm, D_MODEL), F32)],
        compiler_params=_params("parallel", "arbitrary"),
        name="mlp",
    )(x, g1, w_up, w_down, g2)


def _pick(n, pref):
    t = min(n, pref)
    while n % t:
        t //= 2
    return t


def _tiles(B, L):
    T = B * L
    tq, tk = MXU_WIDTH, _pick(L, 2 * MXU_WIDTH)
    unroll = 2 if (L // tk) % 4 == 0 else 1
    groups = _pick(L // tq, 16)
    in_bytes, rest_bytes = _attention_vmem(L, tq, tk, groups, unroll)
    if in_bytes + rest_bytes > ATTENTION_VMEM_BUDGET:
        groups = _pick(L // tq, 8)
    return dict(
        in_proj=dict(tm=_pick(T, 2 * MXU_WIDTH), tn=MAIN_WIDTH),
        mla_prep=dict(tm=_pick(L, 4 * MXU_WIDTH)),
        attention=dict(tq=tq, tk=tk, groups=groups, unroll=unroll),
        gla=dict(cb=_pick(L, 16 * GLA_CHUNK)),
        out_proj=dict(tm=_pick(T, 2 * MXU_WIDTH)),
        mlp=dict(tm=_pick(T, 2 * MXU_WIDTH), tf=4 * MXU_WIDTH),
    )


def _rope_tables(L):
    inv = 1.0 / (ROPE_THETA ** (jnp.arange(0, QK_ROPE, 2, dtype=F32) / QK_ROPE))
    ang = jnp.arange(L, dtype=F32)[:, None] * inv[None, :]
    return jnp.cos(ang), jnp.sin(ang)


def _prepare_weights(w_in, q_a_norm, w_q_b, kv_a_norm, w_kv_b, w_gk_f, b_gk_f, w_gk_b, b_gk_b,
                     gla_norm, w_out, pre_mix_norm, post_mix_norm, pre_mlp_norm, post_mlp_norm,
                     w_up, w_down):
    s = [0]
    for width in (Q_LORA, KV_LORA, QK_ROPE, GLA_QK_WIDTH, GLA_QK_WIDTH, GLA_V_WIDTH,
                  GLA_GATE_RANK, GLA_GATE_RANK, GLA_V_WIDTH):
        s.append(s[-1] + width)
    col = lambda i: w_in[:, s[i]:s[i + 1]]
    w_main = jnp.concatenate([col(0), col(1), col(3), col(4), col(5), col(8)], axis=1).astype(BF16)
    pad = jnp.zeros((D_MODEL, TAIL_WIDTH - QK_ROPE - 2 * GLA_GATE_RANK), w_in.dtype)
    w_tail = jnp.concatenate([col(2), col(6), col(7), pad], axis=1).astype(BF16)
    w_kv = w_kv_b.reshape(KV_LORA, MLA_HEADS, QK_NOPE + V_HEAD)
    row = lambda v: v.reshape(1, -1).astype(F32)
    return dict(
        w_main=w_main, w_tail=w_tail,
        q_norm=row(q_a_norm), kv_norm=row(kv_a_norm),
        wqT=w_q_b.T.astype(BF16),
        wk=w_kv[:, :, :QK_NOPE].reshape(KV_LORA, MLA_HEADS * QK_NOPE).astype(BF16),
        wvT=w_kv[:, :, QK_NOPE:].reshape(KV_LORA, MLA_HEADS * V_HEAD).T.astype(BF16),
        wgf=w_gk_f.astype(F32), bgf=row(b_gk_f), wgb=w_gk_b.astype(F32), bgb=row(b_gk_b),
        gla_norm=row(gla_norm),
        w_out_a=w_out[:MLA_WIDTH].astype(BF16), w_out_b=w_out[MLA_WIDTH:].astype(BF16),
        pre_mix=row(pre_mix_norm), post_mix=row(post_mix_norm),
        pre_mlp=row(pre_mlp_norm), post_mlp=row(post_mlp_norm),
        w_up=w_up.astype(BF16), w_down=w_down.astype(BF16),
    )


def _layer(x, w):
    B, L, _ = x.shape
    T = B * L
    x2 = x.reshape(T, D_MODEL)
    cos, sin = _rope_tables(L)

    t = _tiles(B, L)

    main, tail = _in_proj(x2, w["pre_mix"], w["w_main"], w["w_tail"], **t["in_proj"])
    main = main.reshape(B, L, MAIN_WIDTH)
    tail = tail.reshape(B, L, TAIL_WIDTH)

    qT, k, vT, lgf, lgb = _mla_prep(
        main, tail, w["q_norm"], w["kv_norm"], w["wqT"], w["wk"], w["wvT"],
        w["wgf"], w["bgf"], w["wgb"], w["bgb"], cos, sin, cos.T, sin.T, **t["mla_prep"])
    mla = _attention(qT, k, vT, **t["attention"])

    o_fwd = _gla(main, lgf, None, None, reverse=False, **t["gla"])
    gla = _gla(main, lgb, o_fwd, w["gla_norm"], reverse=True, **t["gla"])

    x2 = _out_proj(x2, mla.reshape(T, MLA_WIDTH), gla.reshape(T, GLA_V_WIDTH),
                   w["w_out_a"], w["w_out_b"], w["post_mix"], **t["out_proj"])
    x2 = _mlp(x2, w["pre_mlp"], w["w_up"], w["w_down"], w["post_mlp"], **t["mlp"])
    return x2.reshape(B, L, D_MODEL)


def kernel(x_prompt, x_sample, w_in, q_a_norm, w_q_b, kv_a_norm, w_kv_b, w_gk_f, b_gk_f, w_gk_b, b_gk_b, gla_norm, w_out, pre_mix_norm, post_mix_norm, pre_mlp_norm, post_mlp_norm, w_up, w_down):
    depth = w_in.shape[0]
    stacked = (w_in, q_a_norm, w_q_b, kv_a_norm, w_kv_b, w_gk_f, b_gk_f, w_gk_b, b_gk_b, gla_norm,
               w_out, pre_mix_norm, post_mix_norm, pre_mlp_norm, post_mlp_norm, w_up, w_down)
    layers = [_prepare_weights(*(t[l] for t in stacked)) for l in range(depth)]
    outs = []
    for x in (x_prompt, x_sample):
        for w in layers:
            x = _layer(x, w)
        outs.append(x)
    return tuple(outs)
```
